```python
import jax, jax.numpy as jnp
from jax import lax
import numpy as np

D_MODEL = 2048
BATCH = 8
SEQ = 4096
DEPTH = 1

HEAD_DIM = 128
N_Q_HEADS = 8
N_KV_HEADS = 2
ATTN_WIDTH = N_Q_HEADS * HEAD_DIM
KV_WIDTH = N_KV_HEADS * HEAD_DIM
WINDOW = 128
ATTN_BLOCK = 128
ROPE_THETA = 10000.0
NEG_INF = -1e30
LRU_WIDTH = D_MODEL - ATTN_WIDTH
LRU_BLOCKS = 8
LRU_BLOCK_W = LRU_WIDTH // LRU_BLOCKS
LRU_C = 8.0
CONV_WIDTH = 4
CONV_PAD_LEFT = 2
CONV_PAD_RIGHT = 1
MIX_WIDTH = ATTN_WIDTH + LRU_WIDTH
IN_WIDTH = ATTN_WIDTH + 2 * KV_WIDTH + 2 * LRU_WIDTH
SPLITS = [ATTN_WIDTH, ATTN_WIDTH + KV_WIDTH, ATTN_WIDTH + 2 * KV_WIDTH,
          ATTN_WIDTH + 2 * KV_WIDTH + LRU_WIDTH]
N_EXPERTS = 32
TOP_K = 4
D_FF_EXPERT = D_MODEL
SWIGLU_ALPHA = 1.702
SWIGLU_LIMIT = 7.0
EXPERT_BLOCK = 512
NORM_EPS = 1e-6

kernel_name = "hymba_swa_rglru_moe_encoder_layer"


def rmsnorm(x, g):
    xf = x.astype(jnp.float32)
    y = xf * lax.rsqrt(jnp.mean(xf * xf, axis=-1, keepdims=True) + NORM_EPS)
    return (y * g.astype(jnp.float32)).astype(x.dtype)


def rotary(x, positions):
    half = HEAD_DIM // 2
    inv_freq = jnp.float32(ROPE_THETA) ** (-jnp.arange(half, dtype=jnp.float32) / half)
    ang = positions.astype(jnp.float32)[:, None] * inv_freq[None, :]
    cos = jnp.cos(ang)[None, :, None, :].astype(x.dtype)
    sin = jnp.sin(ang)[None, :, None, :].astype(x.dtype)
    x1, x2 = x[..., :half], x[..., half:]
    return jnp.concatenate([x1 * cos - x2 * sin, x2 * cos + x1 * sin], axis=-1)


def windowed_attention(q, k, v, g_q, g_k, sink):
    B, S, _ = q.shape
    nb = S // ATTN_BLOCK
    G = N_Q_HEADS // N_KV_HEADS
    pos = jnp.arange(S)
    q = rotary(rmsnorm(q.reshape(B, S, N_Q_HEADS, HEAD_DIM), g_q), pos)
    k = rotary(rmsnorm(k.reshape(B, S, N_KV_HEADS, HEAD_DIM), g_k), pos)
    v = v.reshape(B, S, N_KV_HEADS, HEAD_DIM)
    qb = q.reshape(B, nb, ATTN_BLOCK, N_KV_HEADS, G, HEAD_DIM)

    def band(t):
        tp = jnp.pad(t, ((0, 0), (ATTN_BLOCK, ATTN_BLOCK), (0, 0), (0, 0)))
        tp = tp.reshape(B, nb + 2, ATTN_BLOCK, N_KV_HEADS, HEAD_DIM)
        return jnp.concatenate([tp[:, :-2], tp[:, 1:-1], tp[:, 2:]], axis=2)

    kb, vb = band(k), band(v)
    scores = jnp.einsum('bnqhgd,bnkhd->bnhgqk', qb, kb).astype(jnp.float32) * (HEAD_DIM ** -0.5)
    blk = jnp.arange(nb)[:, None, None]
    qpos = blk * ATTN_BLOCK + jnp.arange(ATTN_BLOCK)[None, :, None]
    kpos = (blk - 1) * ATTN_BLOCK + jnp.arange(3 * ATTN_BLOCK)[None, None, :]
    valid = (jnp.abs(qpos - kpos) <= WINDOW) & (kpos >= 0) & (kpos < S)
    scores = jnp.where(valid[None, :, None, None], scores, NEG_INF)
    sink_logits = jnp.broadcast_to(
        sink.astype(jnp.float32).reshape(1, 1, N_KV_HEADS, G, 1, 1), scores.shape[:-1] + (1,))
    probs = jax.nn.softmax(jnp.concatenate([scores, sink_logits], axis=-1), axis=-1)[..., :-1]
    out = jnp.einsum('bnhgqk,bnkhd->bnqhgd', probs.astype(v.dtype), vb)
    return out.reshape(B, S, ATTN_WIDTH)


def centred_depthwise_conv(x, w, b):
    y = lax.conv_general_dilated(
        x, w[:, None, :].astype(x.dtype), window_strides=(1,),
        padding=[(CONV_PAD_LEFT, CONV_PAD_RIGHT)],
        dimension_numbers=('NWC', 'WIO', 'NWC'), feature_group_count=x.shape[-1])
    return y + b


def _linear_combine(c1, c2):
    a1, b1 = c1
    a2, b2 = c2
    return a1 * a2, a2 * b1 + b2


def rg_lru(x, w_a, b_a, w_x, b_x, lam, reverse):
    B, S, W = x.shape
    xr = x.reshape(B, S, LRU_BLOCKS, LRU_BLOCK_W)
    gate_a = jnp.einsum('bsnc,ncd->bsnd', xr, w_a).reshape(B, S, W) + b_a
    gate_x = jnp.einsum('bsnc,ncd->bsnd', xr, w_x).reshape(B, S, W) + b_x
    r = jax.nn.sigmoid(gate_a.astype(jnp.float32))
    i = jax.nn.sigmoid(gate_x.astype(jnp.float32))
    log_a = -LRU_C * r * jax.nn.softplus(-lam.astype(jnp.float32))
    a = jnp.exp(log_a)
    u = jnp.sqrt(-jnp.expm1(2.0 * log_a)) * (i * x.astype(jnp.float32))
    if reverse:
        a, u = jnp.flip(a, axis=1), jnp.flip(u, axis=1)
    _, h = lax.associative_scan(_linear_combine, (a, u), axis=1)
    if reverse:
        h = jnp.flip(h, axis=1)
    return h.astype(x.dtype)


def clamped_swiglu(gu):
    gate, up = gu[:, :D_FF_EXPERT], gu[:, D_FF_EXPERT:]
    gate = jnp.minimum(gate, SWIGLU_LIMIT)
    up = jnp.clip(up, -SWIGLU_LIMIT, SWIGLU_LIMIT)
    return (up + 1.0) * (gate * jax.nn.sigmoid(SWIGLU_ALPHA * gate))


def moe(h, w_router, b_router, w_gate_up, b_gate_up, w_down, b_down):
    B, S, D = h.shape
    T = B * S
    hf = h.reshape(T, D)
    logits = (hf @ w_router + b_router).astype(jnp.float32)
    top_logits, top_idx = lax.top_k(logits, TOP_K)
    gates = jax.nn.softmax(top_logits, axis=-1).astype(h.dtype)
    A = T * TOP_K
    e_flat = top_idx.reshape(A)
    order = jnp.argsort(e_flat)
    e_sorted = e_flat[order]
    tok_sorted = (order // TOP_K).astype(jnp.int32)
    gate_sorted = gates.reshape(A)[order]
    counts = jnp.bincount(e_flat, length=N_EXPERTS)
    padded = (counts + EXPERT_BLOCK - 1) // EXPERT_BLOCK * EXPERT_BLOCK
    start = jnp.cumsum(counts) - counts
    pend = jnp.cumsum(padded)
    pstart = pend - padded
    dest = pstart[e_sorted] + (jnp.arange(A) - start[e_sorted])
    n_blocks = -(-A // EXPERT_BLOCK) + N_EXPERTS
    R = n_blocks * EXPERT_BLOCK
    row_tok = jnp.full((R,), T, dtype=jnp.int32).at[dest].set(tok_sorted)
    row_gate = jnp.zeros((R,), dtype=h.dtype).at[dest].set(gate_sorted)
    block_expert = jnp.minimum(
        jnp.searchsorted(pend, jnp.arange(n_blocks) * EXPERT_BLOCK, side='right'),
        N_EXPERTS - 1).astype(jnp.int32)
    h_pad = jnp.concatenate([hf, jnp.zeros((1, D), dtype=hf.dtype)], axis=0)

    def step(out, blk):
        rows, g, e = blk
        xb = h_pad[rows]
        gu = xb @ w_gate_up[e] + b_gate_up[e]
        yb = (clamped_swiglu(gu) @ w_down[e] + b_down[e]) * g[:, None]
        return out.at[rows].add(yb.astype(out.dtype)), None

    out, _ = lax.scan(step, jnp.zeros((T + 1, D), dtype=h.dtype),
                      (row_tok.reshape(n_blocks, EXPERT_BLOCK),
                       row_gate.reshape(n_blocks, EXPERT_BLOCK), block_expert))
    return out[:T].reshape(B, S, D)


def setup_inputs(seed: int = 0) -> dict:
    key = jax.random.key(seed)
    ks = jax.random.split(key, 24)
    f32 = jnp.float32

    def nrm(k, shape, scale):
        return jax.random.normal(k, shape, dtype=f32) * scale

    def gain(k, shape):
        return 1.0 + 0.05 * jax.random.normal(k, shape, dtype=f32)

    L = DEPTH
    a0 = jax.random.uniform(ks[13], (L, 2, LRU_WIDTH), dtype=f32, minval=0.9, maxval=0.999)
    p = a0 ** (1.0 / LRU_C)
    lam = jnp.log(p) - jnp.log1p(-p)
    return {
        "x": nrm(ks[0], (BATCH, SEQ, D_MODEL), 1.0),
        "g_norm_mix": gain(ks[1], (L, D_MODEL)),
        "w_in": nrm(ks[2], (L, D_MODEL, IN_WIDTH), D_MODEL ** -0.5),
        "b_in": nrm(ks[3], (L, IN_WIDTH), 0.01),
        "g_q": gain(ks[4], (L, HEAD_DIM)),
        "g_k": gain(ks[5], (L, HEAD_DIM)),
        "sink": nrm(ks[6], (L, N_Q_HEADS), 0.5),
        "conv_w": nrm(ks[7], (L, CONV_WIDTH, LRU_WIDTH), CONV_WIDTH ** -0.5),
        "conv_b": nrm(ks[8], (L, LRU_WIDTH), 0.01),
        "w_rg_a": nrm(ks[9], (L, 2, LRU_BLOCKS, LRU_BLOCK_W, LRU_BLOCK_W), LRU_BLOCK_W ** -0.5),
        "b_rg_a": nrm(ks[10], (L, 2, LRU_WIDTH), 0.01),
        "w_rg_x": nrm(ks[11], (L, 2, LRU_BLOCKS, LRU_BLOCK_W, LRU_BLOCK_W), LRU_BLOCK_W ** -0.5),
        "b_rg_x": nrm(ks[12], (L, 2, LRU_WIDTH), 0.01),
        "lam": lam,
        "g_attn_out": gain(ks[14], (L, ATTN_WIDTH)),
        "g_lru_out": gain(ks[15], (L, LRU_WIDTH)),
        "w_out": nrm(ks[16], (L, MIX_WIDTH, D_MODEL), MIX_WIDTH ** -0.5),
        "g_norm_ffn": gain(ks[17], (L, D_MODEL)),
        "w_router": nrm(ks[18], (L, D_MODEL, N_EXPERTS), D_MODEL ** -0.5),
        "b_router": nrm(ks[19], (L, N_EXPERTS), 0.01),
        "w_gate_up": nrm(ks[20], (L, N_EXPERTS, D_MODEL, 2 * D_FF_EXPERT), D_MODEL ** -0.5),
        "b_gate_up": nrm(ks[21], (L, N_EXPERTS, 2 * D_FF_EXPERT), 0.01),
        "w_down": nrm(ks[22], (L, N_EXPERTS, D_FF_EXPERT, D_MODEL), D_FF_EXPERT ** -0.5),
        "b_down": nrm(ks[23], (L, N_EXPERTS, D_MODEL), 0.01),
    }


def reference(x, g_norm_mix, w_in, b_in, g_q, g_k, sink, conv_w, conv_b,
              w_rg_a, b_rg_a, w_rg_x, b_rg_x, lam, g_attn_out, g_lru_out, w_out,
              g_norm_ffn, w_router, b_router, w_gate_up, b_gate_up, w_down, b_down):
    for l in range(DEPTH):
        hn = rmsnorm(x, g_norm_mix[l])
        proj = hn @ w_in[l] + b_in[l]
        q, k, v, xl, yl = jnp.split(proj, SPLITS, axis=-1)
        attn = windowed_attention(q, k, v, g_q[l], g_k[l], sink[l])
        xc = centred_depthwise_conv(xl, conv_w[l], conv_b[l])
        lru = (rg_lru(xc, w_rg_a[l, 0], b_rg_a[l, 0], w_rg_x[l, 0], b_rg_x[l, 0], lam[l, 0], False)
               + rg_lru(xc, w_rg_a[l, 1], b_rg_a[l, 1], w_rg_x[l, 1], b_rg_x[l, 1], lam[l, 1], True))
        rec = jax.nn.gelu(yl) * lru
        mixed = jnp.concatenate([rmsnorm(attn, g_attn_out[l]), rmsnorm(rec, g_lru_out[l])], axis=-1)
        x = x + mixed @ w_out[l]
        x = x + moe(rmsnorm(x, g_norm_ffn[l]), w_router[l], b_router[l],
                    w_gate_up[l], b_gate_up[l], w_down[l], b_down[l])
    return x
```

```python
import functools

import jax
import jax.numpy as jnp
from jax import lax
from jax.experimental import pallas as pl
from jax.experimental.pallas import tpu as pltpu

F32 = jnp.float32
BF16 = jnp.bfloat16
I32 = jnp.int32

NORM_EPS = 1e-6
HEAD_DIM = 128
N_Q_HEADS = 8
N_KV_HEADS = 2
Q_PER_KV = N_Q_HEADS // N_KV_HEADS
ATTN_BLOCK = 128
WINDOW = 128
ROPE_THETA = 10000.0
NEG_INF = -1e30
LRU_C = 8.0
LRU_BLOCK_W = 128
CONV_WIDTH = 4
TOP_K = 4
SWIGLU_ALPHA = 1.702
SWIGLU_LIMIT = 7.0

LANES = 128
SUBLANES = 8
VMEM_LIMIT = 56 * 1024 * 1024

_PK_IDX, _PK_RANK, _PK_GATE = 0, TOP_K, 2 * TOP_K


def _rms(x, g):
    ms = jnp.mean(x * x, axis=-1, keepdims=True)
    return x * lax.rsqrt(ms + NORM_EPS) * g


def _store_token_slabs(val, ref, row0=0):
    rows, d = val.shape
    nc = d // LANES
    for c in range(nc):
        ref[pl.ds(row0 * nc + c, rows, stride=nc), :] = val[:, c * LANES:(c + 1) * LANES]


def _load_token_chunk(ref, start_row, rows, nc, c):
    return ref[pl.ds(start_row + c, rows, stride=nc), :]


def _slab_copy(src_ref, src_tok, dst_ref, dst_tok, nc, sem):
    return pltpu.make_async_copy(
        src_ref.at[pl.ds(pl.multiple_of(src_tok * nc, nc), nc), :],
        dst_ref.at[pl.ds(pl.multiple_of(dst_tok * nc, nc), nc), :], sem)


def _inproj_body(x_ref, g_ref, w_ref, b_ref, o_ref, hn_ref):
    @pl.when(pl.program_id(1) == 0)
    def _():
        hn_ref[...] = _rms(x_ref[...], g_ref[...]).astype(BF16)

    o_ref[...] = jnp.dot(hn_ref[...], w_ref[...], preferred_element_type=F32) + b_ref[...]


def _inproj(x2d, g, w, b, tm, tn):
    t, d = x2d.shape
    n = w.shape[1]
    return pl.pallas_call(
        _inproj_body,
        grid=(t // tm, n // tn),
        in_specs=[
            pl.BlockSpec((tm, d), lambda i, j: (i, 0)),
            pl.BlockSpec((1, d), lambda i, j: (0, 0)),
            pl.BlockSpec((d, tn), lambda i, j: (0, j)),
            pl.BlockSpec((1, tn), lambda i, j: (0, j)),
        ],
        out_specs=pl.BlockSpec((tm, tn), lambda i, j: (i, j)),
        out_shape=jax.ShapeDtypeStruct((t, n), F32),
        scratch_shapes=[pltpu.VMEM((tm, d), BF16)],
        compiler_params=pltpu.CompilerParams(
            dimension_semantics=("parallel", "arbitrary"), vmem_limit_bytes=VMEM_LIMIT),
        name="inproj",
    )(x2d, g, w, b)


def _attn_body(sink_ref, q_ref, kp_ref, ko_ref, kn_ref, vp_ref, vo_ref, vn_ref,
               cso_ref, csp_ref, csn_ref, gq_ref, gk_ref, o_ref, *, seq):
    n = pl.program_id(1)
    blk = ATTN_BLOCK

    def norm_rope(x, g, cs):
        y = _rms(x, g)
        return y * cs[:, :HEAD_DIM] + pltpu.roll(y, HEAD_DIM // 2, axis=1) * cs[:, HEAD_DIM:]

    gq = gq_ref[...]
    gk = gk_ref[...]
    cso = cso_ref[...]
    k_cs = (csp_ref[...], cso, csn_ref[...])
    k_blocks = (kp_ref, ko_ref, kn_ref)
    v_blocks = (vp_ref, vo_ref, vn_ref)

    rows = lax.broadcasted_iota(I32, (Q_PER_KV * blk, 3 * blk), 0)
    cols = lax.broadcasted_iota(I32, (Q_PER_KV * blk, 3 * blk), 1)
    qpos = n * blk + (rows & (blk - 1))
    kpos = (n - 1) * blk + cols
    valid = (jnp.abs(qpos - kpos) <= WINDOW) & (kpos >= 0) & (kpos < seq)
    head_of_row = lax.shift_right_logical(
        lax.broadcasted_iota(I32, (Q_PER_KV * blk, 1), 0), blk.bit_length() - 1)

    scale = HEAD_DIM ** -0.5
    for h in range(N_KV_HEADS):
        lo = h * HEAD_DIM
        kb = jnp.concatenate(
            [norm_rope(k_blocks[j][:, lo:lo + HEAD_DIM], gk, k_cs[j]) for j in range(3)],
            axis=0).astype(BF16)
        vb = jnp.concatenate([v_blocks[j][:, lo:lo + HEAD_DIM] for j in range(3)],
                             axis=0).astype(BF16)
        qs = []
        sink = jnp.zeros((Q_PER_KV * blk, 1), F32)
        for g in range(Q_PER_KV):
            hq = h * Q_PER_KV + g
            qs.append(norm_rope(q_ref[:, hq * HEAD_DIM:(hq + 1) * HEAD_DIM], gq, cso))
            sink = jnp.where(head_of_row == g, sink_ref[hq], sink)
        qh = jnp.concatenate(qs, axis=0).astype(BF16)
        s = lax.dot_general(qh, kb, (((1,), (1,)), ((), ())),
                            preferred_element_type=F32) * scale
        s = jnp.where(valid, s, NEG_INF)
        m = jnp.maximum(jnp.max(s, axis=-1, keepdims=True), sink)
        p = jnp.exp(s - m)
        denom = jnp.sum(p, axis=-1, keepdims=True) + jnp.exp(sink - m)
        o = jnp.dot(p.astype(BF16), vb, preferred_element_type=F32) / denom
        for g in range(Q_PER_KV):
            hq = h * Q_PER_KV + g
            o_ref[:, hq * HEAD_DIM:(hq + 1) * HEAD_DIM] = o[g * blk:(g + 1) * blk, :]


def _attention(proj, cs, g_q, g_k, sink, batch, seq):
    t = proj.shape[0]
    nb = seq // ATTN_BLOCK
    aw = N_Q_HEADS * HEAD_DIM
    kw = N_KV_HEADS * HEAD_DIM
    k_col = aw // kw
    v_col = k_col + 1

    def krow(off, col):
        return lambda b, n: (b * nb + jnp.clip(n + off, 0, nb - 1), col)

    def csrow(off):
        return lambda b, n: (jnp.clip(n + off, 0, nb - 1), 0)

    return pl.pallas_call(
        functools.partial(_attn_body, seq=seq),
        grid=(batch, nb),
        in_specs=[
            pl.BlockSpec(memory_space=pltpu.SMEM),
            pl.BlockSpec((ATTN_BLOCK, aw), krow(0, 0)),
            pl.BlockSpec((ATTN_BLOCK, kw), krow(-1, k_col)),
            pl.BlockSpec((ATTN_BLOCK, kw), krow(0, k_col)),
            pl.BlockSpec((ATTN_BLOCK, kw), krow(1, k_col)),
            pl.BlockSpec((ATTN_BLOCK, kw), krow(-1, v_col)),
            pl.BlockSpec((ATTN_BLOCK, kw), krow(0, v_col)),
            pl.BlockSpec((ATTN_BLOCK, kw), krow(1, v_col)),
            pl.BlockSpec((ATTN_BLOCK, 2 * HEAD_DIM), csrow(0)),
            pl.BlockSpec((ATTN_BLOCK, 2 * HEAD_DIM), csrow(-1)),
            pl.BlockSpec((ATTN_BLOCK, 2 * HEAD_DIM), csrow(1)),
            pl.BlockSpec((1, HEAD_DIM), lambda b, n: (0, 0)),
            pl.BlockSpec((1, HEAD_DIM), lambda b, n: (0, 0)),
        ],
        out_specs=pl.BlockSpec((ATTN_BLOCK, aw), lambda b, n: (b * nb + n, 0)),
        out_shape=jax.ShapeDtypeStruct((t, aw), F32),
        compiler_params=pltpu.CompilerParams(
            dimension_semantics=("parallel", "arbitrary"), vmem_limit_bytes=VMEM_LIMIT),
        name="attn",
    )(sink, proj, proj, proj, proj, proj, proj, proj, cs, cs, cs, g_q, g_k)


def _rope_table(seq):
    half = HEAD_DIM // 2
    inv_freq = jnp.float32(ROPE_THETA) ** (-jnp.arange(half, dtype=F32) / half)
    ang = jnp.arange(seq).astype(F32)[:, None] * inv_freq[None, :]
    cos, sin = jnp.cos(ang), jnp.sin(ang)
    return jnp.concatenate([cos, cos, -sin, sin], axis=-1)


_LRU_CHUNK = 256
_HALO = SUBLANES


def _softplus(z):
    return jnp.maximum(z, 0.0) + jnp.log1p(jnp.exp(-jnp.abs(z)))


def _lru_body(x_ref, y_ref, cw_ref, cb_ref, w_ref, bias_ref, lam_ref, o_ref,
              xp_ref, af_ref, uf_ref, ab_ref, ub_ref):
    seq = x_ref.shape[0]
    ch = _LRU_CHUNK
    bw = LRU_BLOCK_W

    zero_halo = jnp.zeros((_HALO, bw), F32)
    xp_ref[0:_HALO, :] = zero_halo
    xp_ref[_HALO + seq:2 * _HALO + seq, :] = zero_halo
    xp_ref[_HALO:_HALO + seq, :] = x_ref[...]

    cw = cw_ref[...]
    cb = cb_ref[...]
    bias = bias_ref[0]
    sp = _softplus(-lam_ref[...])
    a_refs = (af_ref, ab_ref)
    u_refs = (uf_ref, ub_ref)

    def gate_chunk(c, carry):
        base = pl.multiple_of(c * ch, ch)
        win = xp_ref[pl.ds(base, ch + 2 * _HALO), :]
        xc = cb
        for j in range(CONV_WIDTH):
            off = _HALO - 2 + j
            xc = xc + cw[j:j + 1, :] * win[off:off + ch, :]
        gates = jnp.dot(xc.astype(BF16), w_ref[0], preferred_element_type=F32) + bias
        for d in range(2):
            ga = gates[:, (2 * d) * bw:(2 * d + 1) * bw]
            gx = gates[:, (2 * d + 1) * bw:(2 * d + 2) * bw]
            r = jax.nn.sigmoid(ga)
            i = jax.nn.sigmoid(gx)
            log_a = (-LRU_C) * r * sp[d:d + 1, :]
            a = jnp.exp(log_a)
            u = jnp.sqrt(1.0 - a * a) * (i * xc)
            a_refs[d][pl.ds(base, ch), :] = a
            u_refs[d][pl.ds(base, ch), :] = u
        return carry

    lax.fori_loop(0, seq // ch, gate_chunk, 0)

    rowi = lax.broadcasted_iota(I32, (SUBLANES, bw), 0)
    nt = seq // SUBLANES

    def tile_scan(a, u, reverse):
        for d in (1, 2, 4):
            if reverse:
                keep = rowi < SUBLANES - d
                shift = SUBLANES - d
            else:
                keep = rowi >= d
                shift = d
            a_s = jnp.where(keep, pltpu.roll(a, shift, axis=0), 1.0)
            u_s = jnp.where(keep, pltpu.roll(u, shift, axis=0), 0.0)
            u = a * u_s + u
            a = a * a_s
        return a, u

    def scan_step(t, carry):
        cf, cbk = carry
        rf = pl.multiple_of(t * SUBLANES, SUBLANES)
        a, u = tile_scan(af_ref[pl.ds(rf, SUBLANES), :], uf_ref[pl.ds(rf, SUBLANES), :], False)
        hf = u + a * cf
        uf_ref[pl.ds(rf, SUBLANES), :] = hf
        cf = jnp.broadcast_to(hf[SUBLANES - 1:SUBLANES, :], (SUBLANES, bw))
        rb = pl.multiple_of((nt - 1 - t) * SUBLANES, SUBLANES)
        a, u = tile_scan(ab_ref[pl.ds(rb, SUBLANES), :], ub_ref[pl.ds(rb, SUBLANES), :], True)
        hb = u + a * cbk
        ub_ref[pl.ds(rb, SUBLANES), :] = hb
        cbk = jnp.broadcast_to(hb[0:1, :], (SUBLANES, bw))
        return cf, cbk

    zero = jnp.zeros((SUBLANES, bw), F32)
    lax.fori_loop(0, nt, scan_step, (zero, zero), unroll=2)

    def out_chunk(c, carry):
        base = pl.multiple_of(c * ch, ch)
        h = uf_ref[pl.ds(base, ch), :] + ub_ref[pl.ds(base, ch), :]
        o_ref[pl.ds(base, ch), :] = jax.nn.gelu(y_ref[pl.ds(base, ch), :]) * h
        return carry

    lax.fori_loop(0, seq // ch, out_chunk, 0)


def _lru(proj, conv_w, conv_b, w_cat, bias_cat, lam, batch, seq, x_col0, y_col0):
    t = proj.shape[0]
    bw = LRU_BLOCK_W
    nblk = conv_w.shape[1] // bw
    xc0 = x_col0 // bw
    yc0 = y_col0 // bw
    return pl.pallas_call(
        _lru_body,
        grid=(batch, nblk),
        in_specs=[
            pl.BlockSpec((seq, bw), lambda b, c: (b, xc0 + c)),
            pl.BlockSpec((seq, bw), lambda b, c: (b, yc0 + c)),
            pl.BlockSpec((CONV_WIDTH, bw), lambda b, c: (0, c)),
            pl.BlockSpec((1, bw), lambda b, c: (0, c)),
            pl.BlockSpec((1, bw, 4 * bw), lambda b, c: (c, 0, 0)),
            pl.BlockSpec((1, 1, 4 * bw), lambda b, c: (c, 0, 0)),
            pl.BlockSpec((2, bw), lambda b, c: (0, c)),
        ],
        out_specs=pl.BlockSpec((seq, bw), lambda b, c: (b, c)),
        out_shape=jax.ShapeDtypeStruct((t, nblk * bw), F32),
        scratch_shapes=[
            pltpu.VMEM((seq + 2 * _HALO, bw), F32),
            pltpu.VMEM((seq, bw), F32),
            pltpu.VMEM((seq, bw), F32),
            pltpu.VMEM((seq, bw), F32),
            pltpu.VMEM((seq, bw), F32),
        ],
        compiler_params=pltpu.CompilerParams(
            dimension_semantics=("parallel", "parallel"), vmem_limit_bytes=VMEM_LIMIT),
        name="lru",
    )(proj, proj, conv_w, conv_b, w_cat, bias_cat, lam)


def _outproj_body(attn_ref, rec_ref, x_ref, ga_ref, gl_ref, wo_ref, gf_ref, wr_ref, br_ref,
                  x2_ref, h2_ref, pk_ref, cnt_ref, carry_ref, *, n_experts):
    tm = x_ref.shape[0]
    aw = attn_ref.shape[1]

    @pl.when(pl.program_id(0) == 0)
    def _():
        carry_ref[...] = jnp.zeros_like(carry_ref)

    an = _rms(attn_ref[...], ga_ref[...]).astype(BF16)
    rn = _rms(rec_ref[...], gl_ref[...]).astype(BF16)
    y = jnp.dot(an, wo_ref[0:aw, :], preferred_element_type=F32)
    y = y + jnp.dot(rn, wo_ref[aw:, :], preferred_element_type=F32)
    x2 = x_ref[...] + y
    x2_ref[...] = x2
    hn = _rms(x2, gf_ref[...])
    _store_token_slabs(hn, h2_ref)

    logits = jnp.dot(hn, wr_ref[...], preferred_element_type=F32,
                     precision=lax.Precision.HIGHEST) + br_ref[...]
    lane = lax.broadcasted_iota(I32, (tm, LANES), 1)
    lane_f = lane.astype(F32)
    work = jnp.where(lane < n_experts, logits, -jnp.inf)

    sels, tops, hots = [], [], []
    for _ in range(TOP_K):
        m = jnp.max(work, axis=1, keepdims=True)
        sel = jnp.min(jnp.where(work == m, lane_f, float(LANES)), axis=1, keepdims=True)
        hot = lane_f == sel
        work = jnp.where(hot, -jnp.inf, work)
        sels.append(sel)
        tops.append(m)
        hots.append(hot)

    exps = [jnp.exp(tk - tops[0]) for tk in tops]
    den = exps[0]
    for e in exps[1:]:
        den = den + e
    gates = [e / den for e in exps]

    chosen = jnp.zeros((tm, LANES), F32)
    for hot in hots:
        chosen = chosen + jnp.where(hot, 1.0, 0.0)
    ri = lax.broadcasted_iota(I32, (tm, tm), 0)
    ci = lax.broadcasted_iota(I32, (tm, tm), 1)
    tri = jnp.where(ci < ri, 1.0, 0.0).astype(BF16)
    before = jnp.dot(tri, chosen.astype(BF16), preferred_element_type=F32) + carry_ref[0:1, :]
    ranks = [jnp.sum(jnp.where(hot, before, 0.0), axis=1, keepdims=True) for hot in hots]

    total = carry_ref[0:1, :] + jnp.sum(chosen, axis=0, keepdims=True)
    carry_ref[...] = jnp.broadcast_to(total, carry_ref.shape)
    cnt_ref[...] = jnp.broadcast_to(total, cnt_ref.shape)

    pk = jnp.zeros((tm, LANES), F32)
    for k in range(TOP_K):
        pk = jnp.where(lane == _PK_IDX + k, sels[k], pk)
        pk = jnp.where(lane == _PK_RANK + k, ranks[k], pk)
        pk = jnp.where(lane == _PK_GATE + k, gates[k], pk)
    pk_ref[...] = pk


def _outproj(attn, rec, x2d, g_attn, g_lru, w_out, g_ffn, w_router, b_router, n_experts, tm):
    t, d = x2d.shape
    aw = attn.shape[1]
    lw = rec.shape[1]
    const = lambda i: (0, 0)
    return pl.pallas_call(
        functools.partial(_outproj_body, n_experts=n_experts),
        grid=(t // tm,),
        in_specs=[
            pl.BlockSpec((tm, aw), lambda i: (i, 0)),
            pl.BlockSpec((tm, lw), lambda i: (i, 0)),
            pl.BlockSpec((tm, d), lambda i: (i, 0)),
            pl.BlockSpec((1, aw), const),
            pl.BlockSpec((1, lw), const),
            pl.BlockSpec((aw + lw, d), const),
            pl.BlockSpec((1, d), const),
            pl.BlockSpec((d, LANES), const),
            pl.BlockSpec((1, LANES), const),
        ],
        out_specs=[
            pl.BlockSpec((tm, d), lambda i: (i, 0)),
            pl.BlockSpec((tm * (d // LANES), LANES), lambda i: (i, 0)),
            pl.BlockSpec((tm, LANES), lambda i: (i, 0)),
            pl.BlockSpec((SUBLANES, LANES), const),
        ],
        out_shape=[
            jax.ShapeDtypeStruct((t, d), F32),
            jax.ShapeDtypeStruct((t * (d // LANES), LANES), F32),
            jax.ShapeDtypeStruct((t, LANES), F32),
            jax.ShapeDtypeStruct((SUBLANES, LANES), F32),
        ],
        scratch_shapes=[pltpu.VMEM((SUBLANES, LANES), F32)],
        compiler_params=pltpu.CompilerParams(
            dimension_semantics=("arbitrary",), vmem_limit_bytes=VMEM_LIMIT),
        name="outproj_router",
    )(attn, rec, x2d, g_attn, g_lru, w_out, g_ffn, w_router, b_router)


def _dispatch_body(dest_ref, h_ref, xs_in_ref, xs_ref, sem_ref, *, tm, nc, steps):
    del xs_in_ref
    i = pl.program_id(0)
    slot = lax.rem(i, 2)

    def drain(s):
        pltpu.make_async_copy(h_ref.at[pl.ds(0, tm * TOP_K * nc), :],
                              xs_ref.at[pl.ds(0, tm * TOP_K * nc), :], sem_ref.at[s]).wait()

    @pl.when(i >= 2)
    def _():
        drain(slot)

    def issue(r, carry):
        tok = i * tm + r
        for k in range(TOP_K):
            _slab_copy(h_ref, tok, xs_ref, dest_ref[0, 0, r * TOP_K + k], nc,
                       sem_ref.at[slot]).start()
        return carry

    lax.fori_loop(0, tm, issue, 0, unroll=4)

    @pl.when(i == steps - 1)
    def _():
        drain(slot)
        if steps >= 2:
            drain(1 - slot)


def _dispatch(dest_flat, h_slabs, n_rows, nc, tm):
    t = h_slabs.shape[0] // nc
    steps = t // tm
    xs_init = jnp.zeros((n_rows * nc, LANES), h_slabs.dtype)
    return pl.pallas_call(
        functools.partial(_dispatch_body, tm=tm, nc=nc, steps=steps),
        grid=(steps,),
        in_specs=[
            pl.BlockSpec((1, 1, tm * TOP_K), lambda i: (i, 0, 0), memory_space=pltpu.SMEM),
            pl.BlockSpec(memory_space=pl.ANY),
            pl.BlockSpec(memory_space=pl.ANY),
        ],
        out_specs=pl.BlockSpec(memory_space=pl.ANY),
        out_shape=jax.ShapeDtypeStruct((n_rows * nc, LANES), h_slabs.dtype),
        scratch_shapes=[pltpu.SemaphoreType.DMA((2,))],
        input_output_aliases={2: 0},
        compiler_params=pltpu.CompilerParams(dimension_semantics=("arbitrary",)),
        name="dispatch",
    )(dest_flat.reshape(steps, 1, tm * TOP_K), h_slabs, xs_init)


def _experts_body(te_ref, nu_ref, xs_ref, wg_ref, wu_ref, bg_ref, bu_ref, wd_ref, bd_ref,
                  o_ref, xb_ref, acc_ref):
    i = pl.program_id(0)
    j = pl.program_id(1)
    nj = pl.num_programs(1)
    tm, d = xb_ref.shape
    nc = d // LANES

    @pl.when(i < nu_ref[0])
    def _():
        @pl.when(j == 0)
        def _():
            for c in range(nc):
                xb_ref[:, c * LANES:(c + 1) * LANES] = _load_token_chunk(
                    xs_ref, 0, tm, nc, c).astype(BF16)

        x = xb_ref[...]
        gate = jnp.dot(x, wg_ref[...], preferred_element_type=F32) + bg_ref[0]
        up = jnp.dot(x, wu_ref[...], preferred_element_type=F32) + bu_ref[0]
        gate = jnp.minimum(gate, SWIGLU_LIMIT)
        up = jnp.clip(up, -SWIGLU_LIMIT, SWIGLU_LIMIT)
        hidden = (up + 1.0) * (gate * jax.nn.sigmoid(SWIGLU_ALPHA * gate))
        part = jnp.dot(hidden.astype(BF16), wd_ref[...], preferred_element_type=F32)

        @pl.when(j == 0)
        def _():
            acc_ref[...] = part

        @pl.when(j > 0)
        def _():
            acc_ref[...] += part

        @pl.when(j == nj - 1)
        def _():
            _store_token_slabs(acc_ref[...] + bd_ref[0], o_ref)

    @pl.when((i >= nu_ref[0]) & (j == nj - 1))
    def _():
        o_ref[...] = jnp.zeros_like(o_ref)


def _experts(tile_expert, n_used, xs, w_gu, b_gu, w_dn, b_dn, tm, fc):
    n_exp, d, two_f = w_gu.shape
    nc = d // LANES
    n_rows = xs.shape[0] // nc
    f = two_f // 2
    nf = f // fc
    n_tiles = n_rows // tm

    def tile(i, nu):
        return jnp.minimum(i, jnp.maximum(nu[0] - 1, 0))

    def chunk(i, j, nu):
        return jnp.where(i < nu[0], j, nf - 1)

    grid_spec = pltpu.PrefetchScalarGridSpec(
        num_scalar_prefetch=2,
        grid=(n_tiles, nf),
        in_specs=[
            pl.BlockSpec((tm * nc, LANES), lambda i, j, te, nu: (tile(i, nu), 0)),
            pl.BlockSpec((None, d, fc), lambda i, j, te, nu: (te[tile(i, nu)], 0, chunk(i, j, nu))),
            pl.BlockSpec((None, d, fc), lambda i, j, te, nu: (te[tile(i, nu)], 0, nf + chunk(i, j, nu))),
            pl.BlockSpec((None, 1, fc), lambda i, j, te, nu: (te[tile(i, nu)], 0, chunk(i, j, nu))),
            pl.BlockSpec((None, 1, fc), lambda i, j, te, nu: (te[tile(i, nu)], 0, nf + chunk(i, j, nu))),
            pl.BlockSpec((None, fc, d), lambda i, j, te, nu: (te[tile(i, nu)], chunk(i, j, nu), 0)),
            pl.BlockSpec((None, 1, d), lambda i, j, te, nu: (te[tile(i, nu)], 0, 0)),
        ],
        out_specs=pl.BlockSpec((tm * nc, LANES), lambda i, j, te, nu: (i, 0)),
        scratch_shapes=[pltpu.VMEM((tm, d), BF16), pltpu.VMEM((tm, d), F32)],
    )
    return pl.pallas_call(
        _experts_body,
        grid_spec=grid_spec,
        out_shape=jax.ShapeDtypeStruct((n_rows * nc, LANES), F32),
        compiler_params=pltpu.CompilerParams(
            dimension_semantics=("arbitrary", "arbitrary"), vmem_limit_bytes=VMEM_LIMIT),
        name="experts",
    )(tile_expert, n_used, xs, w_gu, w_gu, b_gu, b_gu, w_dn, b_dn)


def _combine_body(dcur_ref, dnxt_ref, x2_ref, pk_ref, ys_ref, o_ref, buf_ref, sem_ref,
                  *, tm, steps):
    i = pl.program_id(0)
    slot = lax.rem(i, 2)
    nc = x2_ref.shape[1] // LANES
    slot_rows = TOP_K * tm * nc

    def issue(d_ref, s):
        def body(r, carry):
            for k in range(TOP_K):
                _slab_copy(ys_ref, d_ref[0, 0, r * TOP_K + k], buf_ref, s * (TOP_K * tm) + k * tm + r,
                           nc, sem_ref.at[s]).start()
            return carry

        lax.fori_loop(0, tm, body, 0, unroll=4)

    @pl.when(i == 0)
    def _():
        issue(dcur_ref, 0)

    @pl.when(i + 1 < steps)
    def _():
        issue(dnxt_ref, 1 - slot)

    base = pl.multiple_of(slot * slot_rows, slot_rows)
    pltpu.make_async_copy(ys_ref.at[pl.ds(0, slot_rows), :], buf_ref.at[pl.ds(base, slot_rows), :],
                          sem_ref.at[slot]).wait()

    pk = pk_ref[...]
    gates = [jnp.broadcast_to(pk[:, _PK_GATE + k:_PK_GATE + k + 1], (tm, LANES))
             for k in range(TOP_K)]
    for c in range(nc):
        acc = x2_ref[:, c * LANES:(c + 1) * LANES]
        for k in range(TOP_K):
            acc = acc + gates[k] * _load_token_chunk(buf_ref, base + k * tm * nc, tm, nc, c)
        o_ref[:, c * LANES:(c + 1) * LANES] = acc


def _combine(dest_flat, x2, pk, ys, tm):
    t, d = x2.shape
    nc = d // LANES
    steps = t // tm
    dest3 = dest_flat.reshape(steps, 1, tm * TOP_K)
    return pl.pallas_call(
        functools.partial(_combine_body, tm=tm, steps=steps),
        grid=(steps,),
        in_specs=[
            pl.BlockSpec((1, 1, tm * TOP_K), lambda i: (i, 0, 0), memory_space=pltpu.SMEM),
            pl.BlockSpec((1, 1, tm * TOP_K), lambda i: (jnp.minimum(i + 1, steps - 1), 0, 0),
                         memory_space=pltpu.SMEM),
            pl.BlockSpec((tm, d), lambda i: (i, 0)),
            pl.BlockSpec((tm, LANES), lambda i: (i, 0)),
            pl.BlockSpec(memory_space=pl.ANY),
        ],
        out_specs=pl.BlockSpec((tm, d), lambda i: (i, 0)),
        out_shape=jax.ShapeDtypeStruct((t, d), F32),
        scratch_shapes=[pltpu.VMEM((2 * TOP_K * tm * nc, LANES), F32),
                        pltpu.SemaphoreType.DMA((2,))],
        compiler_params=pltpu.CompilerParams(
            dimension_semantics=("arbitrary",), vmem_limit_bytes=VMEM_LIMIT),
        name="combine",
    )(dest3, dest3, x2, pk, ys)


def _tile_sizes(t):
    def pick(pref):
        while t % pref:
            pref //= 2
        return pref
    return dict(inproj=pick(1024), outproj=pick(512), dispatch=pick(512),
                expert=512, combine=pick(256))


def _layer(x, g_norm_mix, w_in, b_in, g_q, g_k, sink, conv_w, conv_b, w_rg_a, b_rg_a,
           w_rg_x, b_rg_x, lam, g_attn_out, g_lru_out, w_out, g_norm_ffn, w_router, b_router,
           w_gate_up, b_gate_up, w_down, b_down):
    batch, seq, d = x.shape
    t = batch * seq
    ts = _tile_sizes(t)
    x2d = x.reshape(t, d)
    aw = N_Q_HEADS * HEAD_DIM
    kw = N_KV_HEADS * HEAD_DIM
    lw = conv_w.shape[1]
    nblk = lw // LRU_BLOCK_W
    n_exp = w_router.shape[1]

    proj = _inproj(x2d, g_norm_mix[None, :], w_in.astype(BF16), b_in[None, :], ts["inproj"], 512)

    attn = _attention(proj, _rope_table(seq), g_q[None, :], g_k[None, :], sink, batch, seq)

    w_cat = jnp.concatenate([w_rg_a[0], w_rg_x[0], w_rg_a[1], w_rg_x[1]], axis=-1).astype(BF16)
    bias_cat = jnp.concatenate(
        [b.reshape(nblk, 1, LRU_BLOCK_W) for b in (b_rg_a[0], b_rg_x[0], b_rg_a[1], b_rg_x[1])],
        axis=-1)
    rec = _lru(proj, conv_w, conv_b[None, :], w_cat, bias_cat, lam, batch, seq,
               x_col0=aw + 2 * kw, y_col0=aw + 2 * kw + lw)

    w_r = jnp.zeros((d, LANES), F32).at[:, :n_exp].set(w_router)
    b_r = jnp.zeros((1, LANES), F32).at[0, :n_exp].set(b_router)
    x2, h2, pk, cnt = _outproj(attn, rec, x2d, g_attn_out[None, :], g_lru_out[None, :],
                               w_out.astype(BF16), g_norm_ffn[None, :], w_r, b_r, n_exp,
                               ts["outproj"])

    tme = ts["expert"]
    idx = pk[:, _PK_IDX:_PK_IDX + TOP_K].astype(I32)
    rank = pk[:, _PK_RANK:_PK_RANK + TOP_K].astype(I32)
    counts = cnt[0, :n_exp].astype(I32)
    padded = (counts + tme - 1) // tme * tme
    pend = jnp.cumsum(padded)
    pstart = pend - padded
    dest = (pstart[idx] + rank).reshape(t * TOP_K)
    n_tiles = -(-(t * TOP_K) // tme) + n_exp
    tile_expert = jnp.minimum(
        jnp.searchsorted(pend, jnp.arange(n_tiles, dtype=I32) * tme, side="right"),
        n_exp - 1).astype(I32)
    n_used = (pend[-1:] // tme).astype(I32)

    xs = _dispatch(dest, h2, n_tiles * tme, d // LANES, ts["dispatch"])
    f = w_gate_up.shape[-1] // 2
    ys = _experts(tile_expert, n_used, xs, w_gate_up.astype(BF16),
                  b_gate_up.reshape(n_exp, 1, 2 * f), w_down.astype(BF16),
                  b_down.reshape(n_exp, 1, d), tme, 512)
    out = _combine(dest, x2, pk, ys, ts["combine"])
    return out.reshape(batch, seq, d)


def kernel(x, g_norm_mix, w_in, b_in, g_q, g_k, sink, conv_w, conv_b, w_rg_a, b_rg_a, w_rg_x, b_rg_x, lam, g_attn_out, g_lru_out, w_out, g_norm_ffn, w_router, b_router, w_gate_up, b_gate_up, w_down, b_down):
    depth = w_in.shape[0]
    for l in range(depth):
        x = _layer(x, g_norm_mix[l], w_in[l], b_in[l], g_q[l], g_k[l], sink[l], conv_w[l],
                   conv_b[l], w_rg_a[l], b_rg_a[l], w_rg_x[l], b_rg_x[l], lam[l], g_attn_out[l],
                   g_lru_out[l], w_out[l], g_norm_ffn[l], w_router[l], b_router[l],
                   w_gate_up[l], b_gate_up[l], w_down[l], b_down[l])
    return x
```

```python
import functools

import jax
import jax.numpy as jnp
from jax import lax
from jax.experimental import pallas as pl
from jax.experimental.pallas import tpu as pltpu

F32 = jnp.float32
BF16 = jnp.bfloat16
I32 = jnp.int32

NORM_EPS = 1e-6
HEAD_DIM = 128
N_Q_HEADS = 8
N_KV_HEADS = 2
Q_PER_KV = N_Q_HEADS // N_KV_HEADS
ATTN_BLOCK = 128
WINDOW = 128
ROPE_THETA = 10000.0
NEG_INF = -1e30
LRU_C = 8.0
LRU_BLOCK_W = 128
CONV_WIDTH = 4
TOP_K = 4
SWIGLU_ALPHA = 1.702
SWIGLU_LIMIT = 7.0

LANES = 128
SUBLANES = 8
VMEM_LIMIT = 56 * 1024 * 1024

_PK_IDX, _PK_RANK, _PK_GATE = 0, TOP_K, 2 * TOP_K


def _rms(x, g):
    ms = jnp.mean(x * x, axis=-1, keepdims=True)
    return x * lax.rsqrt(ms + NORM_EPS) * g


def _store_token_slabs(val, ref, row0=0):
    rows, d = val.shape
    nc = d // LANES
    for c in range(nc):
        ref[pl.ds(row0 * nc + c, rows, stride=nc), :] = val[:, c * LANES:(c + 1) * LANES]


def _load_token_chunk(ref, start_row, rows, nc, c):
    return ref[pl.ds(start_row + c, rows, stride=nc), :]


def _slab_copy(src_ref, src_tok, dst_ref, dst_tok, nc, sem):
    return pltpu.make_async_copy(
        src_ref.at[pl.ds(pl.multiple_of(src_tok * nc, nc), nc), :],
        dst_ref.at[pl.ds(pl.multiple_of(dst_tok * nc, nc), nc), :], sem)


def _inproj_body(x_ref, g_ref, w_ref, b_ref, o_ref, hn_ref):
    @pl.when(pl.program_id(1) == 0)
    def _():
        hn_ref[...] = _rms(x_ref[...], g_ref[...]).astype(BF16)

    o_ref[...] = jnp.dot(hn_ref[...], w_ref[...], preferred_element_type=F32) + b_ref[...]


def _inproj(x2d, g, w, b, tm, tn):
    t, d = x2d.shape
    n = w.shape[1]
    return pl.pallas_call(
        _inproj_body,
        grid=(t // tm, n // tn),
        in_specs=[
            pl.BlockSpec((tm, d), lambda i, j: (i, 0)),
            pl.BlockSpec((1, d), lambda i, j: (0, 0)),
            pl.BlockSpec((d, tn), lambda i, j: (0, j)),
            pl.BlockSpec((1, tn), lambda i, j: (0, j)),
        ],
        out_specs=pl.BlockSpec((tm, tn), lambda i, j: (i, j)),
        out_shape=jax.ShapeDtypeStruct((t, n), F32),
        scratch_shapes=[pltpu.VMEM((tm, d), BF16)],
        compiler_params=pltpu.CompilerParams(
            dimension_semantics=("parallel", "arbitrary"), vmem_limit_bytes=VMEM_LIMIT),
        name="inproj",
    )(x2d, g, w, b)


def _attn_body(sink_ref, q_ref, kp_ref, ko_ref, kn_ref, vp_ref, vo_ref, vn_ref,
               cso_ref, csp_ref, csn_ref, gq_ref, gk_ref, o_ref, *, seq):
    n = pl.program_id(1)
    blk = ATTN_BLOCK

    def norm_rope(x, g, cs):
        y = _rms(x, g)
        return y * cs[:, :HEAD_DIM] + pltpu.roll(y, HEAD_DIM // 2, axis=1) * cs[:, HEAD_DIM:]

    gq = gq_ref[...]
    gk = gk_ref[...]
    cso = cso_ref[...]
    k_cs = (csp_ref[...], cso, csn_ref[...])
    k_blocks = (kp_ref, ko_ref, kn_ref)
    v_blocks = (vp_ref, vo_ref, vn_ref)

    rows = lax.broadcasted_iota(I32, (Q_PER_KV * blk, 3 * blk), 0)
    cols = lax.broadcasted_iota(I32, (Q_PER_KV * blk, 3 * blk), 1)
    qpos = n * blk + (rows & (blk - 1))
    kpos = (n - 1) * blk + cols
    valid = (jnp.abs(qpos - kpos) <= WINDOW) & (kpos >= 0) & (kpos < seq)
    head_of_row = lax.shift_right_logical(
        lax.broadcasted_iota(I32, (Q_PER_KV * blk, 1), 0), blk.bit_length() - 1)

    scale = HEAD_DIM ** -0.5
    for h in range(N_KV_HEADS):
        lo = h * HEAD_DIM
        kb = jnp.concatenate(
            [norm_rope(k_blocks[j][:, lo:lo + HEAD_DIM], gk, k_cs[j]) for j in range(3)],
            axis=0).astype(BF16)
        vb = jnp.concatenate([v_blocks[j][:, lo:lo + HEAD_DIM] for j in range(3)],
                             axis=0).astype(BF16)
        qs = []
        sink = jnp.zeros((Q_PER_KV * blk, 1), F32)
        for g in range(Q_PER_KV):
            hq = h * Q_PER_KV + g
            qs.append(norm_rope(q_ref[:, hq * HEAD_DIM:(hq + 1) * HEAD_DIM], gq, cso))
            sink = jnp.where(head_of_row == g, sink_ref[hq], sink)
        qh = jnp.concatenate(qs, axis=0).astype(BF16)
        s = lax.dot_general(qh, kb, (((1,), (1,)), ((), ())),
                            preferred_element_type=F32) * scale
        s = jnp.where(valid, s, NEG_INF)
        m = jnp.maximum(jnp.max(s, axis=-1, keepdims=True), sink)
        p = jnp.exp(s - m)
        denom = jnp.sum(p, axis=-1, keepdims=True) + jnp.exp(sink - m)
        o = jnp.dot(p.astype(BF16), vb, preferred_element_type=F32) / denom
        for g in range(Q_PER_KV):
            hq = h * Q_PER_KV + g
            o_ref[:, hq * HEAD_DIM:(hq + 1) * HEAD_DIM] = o[g * blk:(g + 1) * blk, :]


def _attention(proj, cs, g_q, g_k, sink, batch, seq):
    t = proj.shape[0]
    nb = seq // ATTN_BLOCK
    aw = N_Q_HEADS * HEAD_DIM
    kw = N_KV_HEADS * HEAD_DIM
    k_col = aw // kw
    v_col = k_col + 1

    def krow(off, col):
        return lambda b, n: (b * nb + jnp.clip(n + off, 0, nb - 1), col)

    def csrow(off):
        return lambda b, n: (jnp.clip(n + off, 0, nb - 1), 0)

    return pl.pallas_call(
        functools.partial(_attn_body, seq=seq),
        grid=(batch, nb),
        in_specs=[
            pl.BlockSpec(memory_space=pltpu.SMEM),
            pl.BlockSpec((ATTN_BLOCK, aw), krow(0, 0)),
            pl.BlockSpec((ATTN_BLOCK, kw), krow(-1, k_col)),
            pl.BlockSpec((ATTN_BLOCK, kw), krow(0, k_col)),
            pl.BlockSpec((ATTN_BLOCK, kw), krow(1, k_col)),
            pl.BlockSpec((ATTN_BLOCK, kw), krow(-1, v_col)),
            pl.BlockSpec((ATTN_BLOCK, kw), krow(0, v_col)),
            pl.BlockSpec((ATTN_BLOCK, kw), krow(1, v_col)),
            pl.BlockSpec((ATTN_BLOCK, 2 * HEAD_DIM), csrow(0)),
            pl.BlockSpec((ATTN_BLOCK, 2 * HEAD_DIM), csrow(-1)),
            pl.BlockSpec((ATTN_BLOCK, 2 * HEAD_DIM), csrow(1)),
            pl.BlockSpec((1, HEAD_DIM), lambda b, n: (0, 0)),
            pl.BlockSpec((1, HEAD_DIM), lambda b, n: (0, 0)),
        ],
        out_specs=pl.BlockSpec((ATTN_BLOCK, aw), lambda b, n: (b * nb + n, 0)),
        out_shape=jax.ShapeDtypeStruct((t, aw), F32),
        compiler_params=pltpu.CompilerParams(
            dimension_semantics=("parallel", "arbitrary"), vmem_limit_bytes=VMEM_LIMIT),
        name="attn",
    )(sink, proj, proj, proj, proj, proj, proj, proj, cs, cs, cs, g_q, g_k)


def _rope_table(seq):
    half = HEAD_DIM // 2
    inv_freq = jnp.float32(ROPE_THETA) ** (-jnp.arange(half, dtype=F32) / half)
    ang = jnp.arange(seq).astype(F32)[:, None] * inv_freq[None, :]
    cos, sin = jnp.cos(ang), jnp.sin(ang)
    return jnp.concatenate([cos, cos, -sin, sin], axis=-1)


_LRU_CHUNK = 256
_HALO = SUBLANES


def _softplus(z):
    return jnp.maximum(z, 0.0) + jnp.log1p(jnp.exp(-jnp.abs(z)))


def _lru_body(x_ref, y_ref, cw_ref, cb_ref, w_ref, bias_ref, lam_ref, o_ref,
              xp_ref, af_ref, uf_ref, ab_ref, ub_ref, hf_ref, hb_ref):
    seq = x_ref.shape[0]
    ch = _LRU_CHUNK
    bw = LRU_BLOCK_W

    zero_halo = jnp.zeros((_HALO, bw), F32)
    xp_ref[0:_HALO, :] = zero_halo
    xp_ref[_HALO + seq:2 * _HALO + seq, :] = zero_halo
    xp_ref[_HALO:_HALO + seq, :] = x_ref[...]

    cw = cw_ref[...]
    cb = cb_ref[...]
    bias = bias_ref[0]
    sp = _softplus(-lam_ref[...])
    a_refs = (af_ref, ab_ref)
    u_refs = (uf_ref, ub_ref)

    def gate_chunk(c, carry):
        base = pl.multiple_of(c * ch, ch)
        win = xp_ref[pl.ds(base, ch + 2 * _HALO), :]
        xc = cb
        for j in range(CONV_WIDTH):
            off = _HALO - 2 + j
            xc = xc + cw[j:j + 1, :] * win[off:off + ch, :]
        gates = jnp.dot(xc.astype(BF16), w_ref[0], preferred_element_type=F32) + bias
        for d in range(2):
            ga = gates[:, (2 * d) * bw:(2 * d + 1) * bw]
            gx = gates[:, (2 * d + 1) * bw:(2 * d + 2) * bw]
            r = jax.nn.sigmoid(ga)
            i = jax.nn.sigmoid(gx)
            log_a = (-LRU_C) * r * sp[d:d + 1, :]
            a = jnp.exp(log_a)
            u = jnp.sqrt(1.0 - a * a) * (i * xc)
            a_refs[d][pl.ds(base, ch), :] = a
            u_refs[d][pl.ds(base, ch), :] = u
        return carry

    lax.fori_loop(0, seq // ch, gate_chunk, 0)

    rowi = lax.broadcasted_iota(I32, (SUBLANES, bw), 0)
    nt = seq // SUBLANES

    def tile_scan(a, u, reverse):
        for d in (1, 2, 4):
            if reverse:
                keep = rowi < SUBLANES - d
                shift = SUBLANES - d
            else:
                keep = rowi >= d
                shift = d
            a_s = jnp.where(keep, pltpu.roll(a, shift, axis=0), 1.0)
            u_s = jnp.where(keep, pltpu.roll(u, shift, axis=0), 0.0)
            u = a * u_s + u
            a = a * a_s
        return a, u

    def bcast_row(x, r):
        return jnp.broadcast_to(x[r:r + 1, :], (SUBLANES, bw))

    def scan_step(t, carry):
        cf, cbk = carry
        rf = pl.multiple_of(t * SUBLANES, SUBLANES)
        a, u = tile_scan(af_ref[pl.ds(rf, SUBLANES), :], uf_ref[pl.ds(rf, SUBLANES), :], False)
        hf_ref[pl.ds(rf, SUBLANES), :] = u + a * cf
        cf = bcast_row(u, SUBLANES - 1) + bcast_row(a, SUBLANES - 1) * cf
        rb = pl.multiple_of((nt - 1 - t) * SUBLANES, SUBLANES)
        a, u = tile_scan(ab_ref[pl.ds(rb, SUBLANES), :], ub_ref[pl.ds(rb, SUBLANES), :], True)
        hb_ref[pl.ds(rb, SUBLANES), :] = u + a * cbk
        cbk = bcast_row(u, 0) + bcast_row(a, 0) * cbk
        return cf, cbk

    zero = jnp.zeros((SUBLANES, bw), F32)
    lax.fori_loop(0, nt, scan_step, (zero, zero), unroll=4)

    def out_chunk(c, carry):
        base = pl.multiple_of(c * ch, ch)
        h = hf_ref[pl.ds(base, ch), :] + hb_ref[pl.ds(base, ch), :]
        o_ref[pl.ds(base, ch), :] = jax.nn.gelu(y_ref[pl.ds(base, ch), :]) * h
        return carry

    lax.fori_loop(0, seq // ch, out_chunk, 0)


def _lru(proj, conv_w, conv_b, w_cat, bias_cat, lam, batch, seq, x_col0, y_col0):
    t = proj.shape[0]
    bw = LRU_BLOCK_W
    nblk = conv_w.shape[1] // bw
    xc0 = x_col0 // bw
    yc0 = y_col0 // bw
    return pl.pallas_call(
        _lru_body,
        grid=(batch, nblk),
        in_specs=[
            pl.BlockSpec((seq, bw), lambda b, c: (b, xc0 + c)),
            pl.BlockSpec((seq, bw), lambda b, c: (b, yc0 + c)),
            pl.BlockSpec((CONV_WIDTH, bw), lambda b, c: (0, c)),
            pl.BlockSpec((1, bw), lambda b, c: (0, c)),
            pl.BlockSpec((1, bw, 4 * bw), lambda b, c: (c, 0, 0)),
            pl.BlockSpec((1, 1, 4 * bw), lambda b, c: (c, 0, 0)),
            pl.BlockSpec((2, bw), lambda b, c: (0, c)),
        ],
        out_specs=pl.BlockSpec((seq, bw), lambda b, c: (b, c)),
        out_shape=jax.ShapeDtypeStruct((t, nblk * bw), F32),
        scratch_shapes=[
            pltpu.VMEM((seq + 2 * _HALO, bw), F32),
            pltpu.VMEM((seq, bw), F32),
            pltpu.VMEM((seq, bw), F32),
            pltpu.VMEM((seq, bw), F32),
            pltpu.VMEM((seq, bw), F32),
            pltpu.VMEM((seq, bw), F32),
            pltpu.VMEM((seq, bw), F32),
        ],
        compiler_params=pltpu.CompilerParams(
            dimension_semantics=("parallel", "parallel"), vmem_limit_bytes=VMEM_LIMIT),
        name="lru",
    )(proj, proj, conv_w, conv_b, w_cat, bias_cat, lam)


def _outproj_body(attn_ref, rec_ref, x_ref, ga_ref, gl_ref, wo_ref, gf_ref, wr_ref, br_ref,
                  x2_ref, h2_ref, pk_ref, cnt_ref, carry_ref, *, n_experts):
    tm = x_ref.shape[0]
    aw = attn_ref.shape[1]

    @pl.when(pl.program_id(0) == 0)
    def _():
        carry_ref[...] = jnp.zeros_like(carry_ref)

    an = _rms(attn_ref[...], ga_ref[...]).astype(BF16)
    rn = _rms(rec_ref[...], gl_ref[...]).astype(BF16)
    y = jnp.dot(an, wo_ref[0:aw, :], preferred_element_type=F32)
    y = y + jnp.dot(rn, wo_ref[aw:, :], preferred_element_type=F32)
    x2 = x_ref[...] + y
    x2_ref[...] = x2
    hn = _rms(x2, gf_ref[...])
    _store_token_slabs(hn, h2_ref)

    hn_hi = hn.astype(BF16)
    hn_lo = (hn - hn_hi.astype(F32)).astype(BF16)
    hi_part = jnp.dot(hn_hi, wr_ref[...], preferred_element_type=F32)
    lo_part = jnp.dot(hn_lo, wr_ref[:, 0:LANES], preferred_element_type=F32)
    logits = hi_part[:, 0:LANES] + (hi_part[:, LANES:] + lo_part) + br_ref[...]
    lane = lax.broadcasted_iota(I32, (tm, LANES), 1)
    lane_f = lane.astype(F32)
    work = jnp.where(lane < n_experts, logits, -jnp.inf)

    sels, tops, hots = [], [], []
    for _ in range(TOP_K):
        m = jnp.max(work, axis=1, keepdims=True)
        sel = jnp.min(jnp.where(work == m, lane_f, float(LANES)), axis=1, keepdims=True)
        hot = lane_f == sel
        work = jnp.where(hot, -jnp.inf, work)
        sels.append(sel)
        tops.append(m)
        hots.append(hot)

    exps = [jnp.exp(tk - tops[0]) for tk in tops]
    den = exps[0]
    for e in exps[1:]:
        den = den + e
    gates = [e / den for e in exps]

    chosen = jnp.zeros((tm, LANES), F32)
    for hot in hots:
        chosen = chosen + jnp.where(hot, 1.0, 0.0)
    ri = lax.broadcasted_iota(I32, (tm, tm), 0)
    ci = lax.broadcasted_iota(I32, (tm, tm), 1)
    tri = jnp.where(ci < ri, 1.0, 0.0).astype(BF16)
    before = jnp.dot(tri, chosen.astype(BF16), preferred_element_type=F32) + carry_ref[0:1, :]
    ranks = [jnp.sum(jnp.where(hot, before, 0.0), axis=1, keepdims=True) for hot in hots]

    total = carry_ref[0:1, :] + jnp.sum(chosen, axis=0, keepdims=True)
    carry_ref[...] = jnp.broadcast_to(total, carry_ref.shape)
    cnt_ref[...] = jnp.broadcast_to(total, cnt_ref.shape)

    pk = jnp.zeros((tm, LANES), F32)
    for k in range(TOP_K):
        pk = jnp.where(lane == _PK_IDX + k, sels[k], pk)
        pk = jnp.where(lane == _PK_RANK + k, ranks[k], pk)
        pk = jnp.where(lane == _PK_GATE + k, gates[k], pk)
    pk_ref[...] = pk


def _outproj(attn, rec, x2d, g_attn, g_lru, w_out, g_ffn, w_router, b_router, n_experts, tm):
    t, d = x2d.shape
    aw = attn.shape[1]
    lw = rec.shape[1]
    const = lambda i: (0, 0)
    return pl.pallas_call(
        functools.partial(_outproj_body, n_experts=n_experts),
        grid=(t // tm,),
        in_specs=[
            pl.BlockSpec((tm, aw), lambda i: (i, 0)),
            pl.BlockSpec((tm, lw), lambda i: (i, 0)),
            pl.BlockSpec((tm, d), lambda i: (i, 0)),
            pl.BlockSpec((1, aw), const),
            pl.BlockSpec((1, lw), const),
            pl.BlockSpec((aw + lw, d), const),
            pl.BlockSpec((1, d), const),
            pl.BlockSpec((d, 2 * LANES), const),
            pl.BlockSpec((1, LANES), const),
        ],
        out_specs=[
            pl.BlockSpec((tm, d), lambda i: (i, 0)),
            pl.BlockSpec((tm * (d // LANES), LANES), lambda i: (i, 0)),
            pl.BlockSpec((tm, LANES), lambda i: (i, 0)),
            pl.BlockSpec((SUBLANES, LANES), const),
        ],
        out_shape=[
            jax.ShapeDtypeStruct((t, d), F32),
            jax.ShapeDtypeStruct((t * (d // LANES), LANES), F32),
            jax.ShapeDtypeStruct((t, LANES), F32),
            jax.ShapeDtypeStruct((SUBLANES, LANES), F32),
        ],
        scratch_shapes=[pltpu.VMEM((SUBLANES, LANES), F32)],
        compiler_params=pltpu.CompilerParams(
            dimension_semantics=("arbitrary",), vmem_limit_bytes=VMEM_LIMIT),
        name="outproj_router",
    )(attn, rec, x2d, g_attn, g_lru, w_out, g_ffn, w_router, b_router)


def _experts_body(te_ref, nu_ref, scur_ref, snxt_ref, dcur_ref, h_ref, wg_ref, wu_ref, bg_ref,
                  bu_ref, wd_ref, bd_ref, y_ref, raw_ref, xb_ref, acc_ref, yst_ref, gsem_ref,
                  ssem_ref, *, n_tok, n_tiles):
    i = pl.program_id(0)
    j = pl.program_id(1)
    nj = pl.num_programs(1)
    tm, d = xb_ref.shape
    nc = d // LANES
    nu = nu_ref[0]
    slot = lax.rem(i, 2)
    dummy = n_tok * TOP_K

    def gather(s_ref, s):
        def body(r, carry):
            _slab_copy(h_ref, s_ref[0, 0, r], raw_ref, s * tm + r, nc, gsem_ref.at[s]).start()
            return carry

        lax.fori_loop(0, tm, body, 0, unroll=8)

    def scatter_wait():
        pltpu.make_async_copy(yst_ref, y_ref.at[pl.ds(0, tm * nc), :], ssem_ref.at[0]).wait()

    @pl.when((i == 0) & (j == 0))
    def _():
        yst_ref[...] = jnp.zeros_like(yst_ref)
        cp = pltpu.make_async_copy(yst_ref, y_ref.at[pl.ds(dummy * nc, tm * nc), :],
                                   ssem_ref.at[1])
        cp.start()
        cp.wait()

        @pl.when(nu > 0)
        def _():
            gather(scur_ref, 0)

    @pl.when((j == 0) & (i + 1 < nu))
    def _():
        gather(snxt_ref, 1 - slot)

    @pl.when(i < nu)
    def _():
        @pl.when(j == 0)
        def _():
            base = pl.multiple_of(slot * (tm * nc), tm * nc)
            pltpu.make_async_copy(h_ref.at[pl.ds(0, tm * nc), :],
                                  raw_ref.at[pl.ds(base, tm * nc), :], gsem_ref.at[slot]).wait()
            for c in range(nc):
                xb_ref[:, c * LANES:(c + 1) * LANES] = _load_token_chunk(
                    raw_ref, base, tm, nc, c).astype(BF16)

        x = xb_ref[...]
        gate = jnp.dot(x, wg_ref[...], preferred_element_type=F32) + bg_ref[0]
        up = jnp.dot(x, wu_ref[...], preferred_element_type=F32) + bu_ref[0]
        gate = jnp.minimum(gate, SWIGLU_LIMIT)
        up = jnp.clip(up, -SWIGLU_LIMIT, SWIGLU_LIMIT)
        hidden = (up + 1.0) * (gate * jax.nn.sigmoid(SWIGLU_ALPHA * gate))
        part = jnp.dot(hidden.astype(BF16), wd_ref[...], preferred_element_type=F32)
        acc_ref[...] = jnp.where(j > 0, acc_ref[...], 0.0) + part

        @pl.when(j == nj - 1)
        def _():
            @pl.when(i > 0)
            def _():
                scatter_wait()

            _store_token_slabs(acc_ref[...] + bd_ref[0], yst_ref)

            def body(r, carry):
                _slab_copy(yst_ref, r, y_ref, dcur_ref[0, 0, r], nc, ssem_ref.at[0]).start()
                return carry

            lax.fori_loop(0, tm, body, 0, unroll=8)

    @pl.when((i == n_tiles - 1) & (j == nj - 1) & (nu > 0))
    def _():
        scatter_wait()


def _experts(tile_expert, n_used, row_asg, h_slabs, w_gu, b_gu, w_dn, b_dn, tm, fc):
    n_exp, d, two_f = w_gu.shape
    nc = d // LANES
    n_tok = h_slabs.shape[0] // nc
    f = two_f // 2
    nf = f // fc
    n_tiles = row_asg.shape[0] // tm
    pad = row_asg >= n_tok * TOP_K
    row_tok = row_asg // TOP_K
    src3 = jnp.where(pad, 0, row_tok).reshape(n_tiles, 1, tm)
    spare = n_tok * TOP_K + jnp.arange(n_tiles * tm, dtype=I32) % tm
    dst3 = jnp.where(pad, spare, (row_asg % TOP_K) * n_tok + row_tok).reshape(n_tiles, 1, tm)

    def tile(i, nu):
        return jnp.minimum(i, jnp.maximum(nu[0] - 1, 0))

    def chunk(i, j, nu):
        return jnp.where(i < nu[0], j, nf - 1)

    grid_spec = pltpu.PrefetchScalarGridSpec(
        num_scalar_prefetch=2,
        grid=(n_tiles, nf),
        in_specs=[
            pl.BlockSpec((1, 1, tm), lambda i, j, te, nu: (tile(i, nu), 0, 0),
                         memory_space=pltpu.SMEM),
            pl.BlockSpec((1, 1, tm), lambda i, j, te, nu: (tile(i + 1, nu), 0, 0),
                         memory_space=pltpu.SMEM),
            pl.BlockSpec((1, 1, tm), lambda i, j, te, nu: (tile(i, nu), 0, 0),
                         memory_space=pltpu.SMEM),
            pl.BlockSpec(memory_space=pl.ANY),
            pl.BlockSpec((None, d, fc), lambda i, j, te, nu: (te[tile(i, nu)], 0, chunk(i, j, nu))),
            pl.BlockSpec((None, d, fc), lambda i, j, te, nu: (te[tile(i, nu)], 0, nf + chunk(i, j, nu))),
            pl.BlockSpec((None, 1, fc), lambda i, j, te, nu: (te[tile(i, nu)], 0, chunk(i, j, nu))),
            pl.BlockSpec((None, 1, fc), lambda i, j, te, nu: (te[tile(i, nu)], 0, nf + chunk(i, j, nu))),
            pl.BlockSpec((None, fc, d), lambda i, j, te, nu: (te[tile(i, nu)], chunk(i, j, nu), 0)),
            pl.BlockSpec((None, 1, d), lambda i, j, te, nu: (te[tile(i, nu)], 0, 0)),
        ],
        out_specs=pl.BlockSpec(memory_space=pl.ANY),
        scratch_shapes=[
            pltpu.VMEM((2 * tm * nc, LANES), F32),
            pltpu.VMEM((tm, d), BF16),
            pltpu.VMEM((tm, d), F32),
            pltpu.VMEM((tm * nc, LANES), F32),
            pltpu.SemaphoreType.DMA((2,)),
            pltpu.SemaphoreType.DMA((2,)),
        ],
    )
    return pl.pallas_call(
        functools.partial(_experts_body, n_tok=n_tok, n_tiles=n_tiles),
        grid_spec=grid_spec,
        out_shape=jax.ShapeDtypeStruct(((n_tok * TOP_K + tm) * nc, LANES), F32),
        compiler_params=pltpu.CompilerParams(
            dimension_semantics=("arbitrary", "arbitrary"), vmem_limit_bytes=VMEM_LIMIT),
        name="experts",
    )(tile_expert, n_used, src3, src3, dst3, h_slabs, w_gu, w_gu, b_gu, b_gu, w_dn, b_dn)


def _combine_body(x2_ref, pk_ref, *rest):
    y_refs, o_ref = rest[:TOP_K], rest[TOP_K]
    tm, d = x2_ref.shape
    nc = d // LANES
    pk = pk_ref[...]
    gates = [jnp.broadcast_to(pk[:, _PK_GATE + k:_PK_GATE + k + 1], (tm, LANES))
             for k in range(TOP_K)]
    for c in range(nc):
        acc = x2_ref[:, c * LANES:(c + 1) * LANES]
        for k in range(TOP_K):
            acc = acc + gates[k] * _load_token_chunk(y_refs[k], 0, tm, nc, c)
        o_ref[:, c * LANES:(c + 1) * LANES] = acc


def _combine(x2, pk, y, tm):
    t, d = x2.shape
    nc = d // LANES
    steps = t // tm

    def y_spec(k):
        return pl.BlockSpec((tm * nc, LANES), lambda i: (k * steps + i, 0))

    return pl.pallas_call(
        _combine_body,
        grid=(steps,),
        in_specs=[
            pl.BlockSpec((tm, d), lambda i: (i, 0)),
            pl.BlockSpec((tm, LANES), lambda i: (i, 0)),
        ] + [y_spec(k) for k in range(TOP_K)],
        out_specs=pl.BlockSpec((tm, d), lambda i: (i, 0)),
        out_shape=jax.ShapeDtypeStruct((t, d), F32),
        compiler_params=pltpu.CompilerParams(
            dimension_semantics=("parallel",), vmem_limit_bytes=VMEM_LIMIT),
        name="combine",
    )(x2, pk, *([y] * TOP_K))


def _tile_sizes(t):
    def pick(pref):
        while t % pref:
            pref //= 2
        return pref
    return dict(inproj=pick(1024), outproj=pick(512), expert=512, ffn_chunk=1024,
                combine=pick(256))


def _layer(x, g_norm_mix, w_in, b_in, g_q, g_k, sink, conv_w, conv_b, w_rg_a, b_rg_a,
           w_rg_x, b_rg_x, lam, g_attn_out, g_lru_out, w_out, g_norm_ffn, w_router, b_router,
           w_gate_up, b_gate_up, w_down, b_down):
    batch, seq, d = x.shape
    t = batch * seq
    ts = _tile_sizes(t)
    x2d = x.reshape(t, d)
    aw = N_Q_HEADS * HEAD_DIM
    kw = N_KV_HEADS * HEAD_DIM
    lw = conv_w.shape[1]
    nblk = lw // LRU_BLOCK_W
    n_exp = w_router.shape[1]

    proj = _inproj(x2d, g_norm_mix[None, :], w_in.astype(BF16), b_in[None, :], ts["inproj"], 512)

    attn = _attention(proj, _rope_table(seq), g_q[None, :], g_k[None, :], sink, batch, seq)

    w_cat = jnp.concatenate([w_rg_a[0], w_rg_x[0], w_rg_a[1], w_rg_x[1]], axis=-1).astype(BF16)
    bias_cat = jnp.concatenate(
        [b.reshape(nblk, 1, LRU_BLOCK_W) for b in (b_rg_a[0], b_rg_x[0], b_rg_a[1], b_rg_x[1])],
        axis=-1)
    rec = _lru(proj, conv_w, conv_b[None, :], w_cat, bias_cat, lam, batch, seq,
               x_col0=aw + 2 * kw, y_col0=aw + 2 * kw + lw)

    w_r = jnp.zeros((d, LANES), F32).at[:, :n_exp].set(w_router)
    w_r_hi = w_r.astype(BF16)
    w_r = jnp.concatenate([w_r_hi, (w_r - w_r_hi.astype(F32)).astype(BF16)], axis=1)
    b_r = jnp.zeros((1, LANES), F32).at[0, :n_exp].set(b_router)
    x2, h2, pk, cnt = _outproj(attn, rec, x2d, g_attn_out[None, :], g_lru_out[None, :],
                               w_out.astype(BF16), g_norm_ffn[None, :], w_r, b_r, n_exp,
                               ts["outproj"])

    tme = ts["expert"]
    idx = pk[:, _PK_IDX:_PK_IDX + TOP_K].astype(I32)
    rank = pk[:, _PK_RANK:_PK_RANK + TOP_K].astype(I32)
    counts = cnt[0, :n_exp].astype(I32)
    padded = (counts + tme - 1) // tme * tme
    pend = jnp.cumsum(padded)
    pstart = pend - padded
    dest = (pstart[idx] + rank).reshape(t * TOP_K)
    n_tiles = -(-(t * TOP_K) // tme) + n_exp
    tile_expert = jnp.minimum(
        jnp.searchsorted(pend, jnp.arange(n_tiles, dtype=I32) * tme, side="right"),
        n_exp - 1).astype(I32)
    n_used = (pend[-1:] // tme).astype(I32)
    row_asg = jnp.full((n_tiles * tme,), t * TOP_K, I32).at[dest].set(
        jnp.arange(t * TOP_K, dtype=I32))

    f = w_gate_up.shape[-1] // 2
    y = _experts(tile_expert, n_used, row_asg, h2, w_gate_up.astype(BF16),
                 b_gate_up.reshape(n_exp, 1, 2 * f), w_down.astype(BF16),
                 b_down.reshape(n_exp, 1, d), tme, ts["ffn_chunk"])
    out = _combine(x2, pk, y, ts["combine"])
    return out.reshape(batch, seq, d)


def kernel(x, g_norm_mix, w_in, b_in, g_q, g_k, sink, conv_w, conv_b, w_rg_a, b_rg_a, w_rg_x, b_rg_x, lam, g_attn_out, g_lru_out, w_out, g_norm_ffn, w_router, b_router, w_gate_up, b_gate_up, w_down, b_down):
    depth = w_in.shape[0]
    for l in range(depth):
        x = _layer(x, g_norm_mix[l], w_in[l], b_in[l], g_q[l], g_k[l], sink[l], conv_w[l],
                   conv_b[l], w_rg_a[l], b_rg_a[l], w_rg_x[l], b_rg_x[l], lam[l], g_attn_out[l],
                   g_lru_out[l], w_out[l], g_norm_ffn[l], w_router[l], b_router[l],
                   w_gate_up[l], b_gate_up[l], w_down[l], b_down[l])
    return x
```

```python
import functools

import jax
import jax.numpy as jnp
from jax import lax
from jax.experimental import pallas as pl
from jax.experimental.pallas import tpu as pltpu

F32 = jnp.float32
BF16 = jnp.bfloat16
I32 = jnp.int32

NORM_EPS = 1e-6
HEAD_DIM = 128
N_Q_HEADS = 8
N_KV_HEADS = 2
Q_PER_KV = N_Q_HEADS // N_KV_HEADS
ATTN_BLOCK = 128
WINDOW = 128
ROPE_THETA = 10000.0
NEG_INF = -1e30
LRU_C = 8.0
LRU_BLOCK_W = 128
CONV_WIDTH = 4
TOP_K = 4
SWIGLU_ALPHA = 1.702
SWIGLU_LIMIT = 7.0

LANES = 128
SUBLANES = 8
VMEM_LIMIT = 56 * 1024 * 1024
EXPERTS_VMEM_LIMIT = 62 * 1024 * 1024

_PK_IDX, _PK_RANK, _PK_GATE = 0, TOP_K, 2 * TOP_K


def _rms(x, g):
    ms = jnp.mean(x * x, axis=-1, keepdims=True)
    return x * lax.rsqrt(ms + NORM_EPS) * g


def _store_token_slabs(val, ref, row0=0):
    rows, d = val.shape
    nc = d // LANES
    for c in range(nc):
        ref[pl.ds(row0 * nc + c, rows, stride=nc), :] = val[:, c * LANES:(c + 1) * LANES]


def _load_token_chunk(ref, start_row, rows, nc, c):
    return ref[pl.ds(start_row + c, rows, stride=nc), :]


def _slab_copy(src_ref, src_tok, dst_ref, dst_tok, nc, sem):
    return pltpu.make_async_copy(
        src_ref.at[pl.ds(pl.multiple_of(src_tok * nc, nc), nc), :],
        dst_ref.at[pl.ds(pl.multiple_of(dst_tok * nc, nc), nc), :], sem)


def _inproj_body(x_ref, g_ref, w_ref, b_ref, o_ref, hn_ref):
    @pl.when(pl.program_id(1) == 0)
    def _():
        hn_ref[...] = _rms(x_ref[...], g_ref[...]).astype(BF16)

    o_ref[...] = jnp.dot(hn_ref[...], w_ref[...], preferred_element_type=F32) + b_ref[...]


def _inproj(x2d, g, w, b, tm, tn):
    t, d = x2d.shape
    n = w.shape[1]
    return pl.pallas_call(
        _inproj_body,
        grid=(t // tm, n // tn),
        in_specs=[
            pl.BlockSpec((tm, d), lambda i, j: (i, 0)),
            pl.BlockSpec((1, d), lambda i, j: (0, 0)),
            pl.BlockSpec((d, tn), lambda i, j: (0, j)),
            pl.BlockSpec((1, tn), lambda i, j: (0, j)),
        ],
        out_specs=pl.BlockSpec((tm, tn), lambda i, j: (i, j)),
        out_shape=jax.ShapeDtypeStruct((t, n), F32),
        scratch_shapes=[pltpu.VMEM((tm, d), BF16)],
        compiler_params=pltpu.CompilerParams(
            dimension_semantics=("parallel", "arbitrary"), vmem_limit_bytes=VMEM_LIMIT),
        name="inproj",
    )(x2d, g, w, b)


def _attn_body(sink_ref, q_ref, kp_ref, ko_ref, kn_ref, vp_ref, vo_ref, vn_ref,
               cso_ref, csp_ref, csn_ref, gq_ref, gk_ref, wgu_ref, wdn_ref,
               o_ref, wgu_o_ref, wdn_o_ref, *, seq):
    n = pl.program_id(1)
    blk = ATTN_BLOCK

    wgu_o_ref[...] = wgu_ref[...].astype(BF16)
    wdn_o_ref[...] = wdn_ref[...].astype(BF16)

    def norm_rope(x, g, cs):
        y = _rms(x, g)
        return y * cs[:, :HEAD_DIM] + pltpu.roll(y, HEAD_DIM // 2, axis=1) * cs[:, HEAD_DIM:]

    gq = gq_ref[...]
    gk = gk_ref[...]
    cso = cso_ref[...]
    k_cs = (csp_ref[...], cso, csn_ref[...])
    k_blocks = (kp_ref, ko_ref, kn_ref)
    v_blocks = (vp_ref, vo_ref, vn_ref)

    rows = lax.broadcasted_iota(I32, (Q_PER_KV * blk, 3 * blk), 0)
    cols = lax.broadcasted_iota(I32, (Q_PER_KV * blk, 3 * blk), 1)
    qpos = n * blk + (rows & (blk - 1))
    kpos = (n - 1) * blk + cols
    valid = (jnp.abs(qpos - kpos) <= WINDOW) & (kpos >= 0) & (kpos < seq)
    head_of_row = lax.shift_right_logical(
        lax.broadcasted_iota(I32, (Q_PER_KV * blk, 1), 0), blk.bit_length() - 1)

    scale = HEAD_DIM ** -0.5
    for h in range(N_KV_HEADS):
        lo = h * HEAD_DIM
        kb = jnp.concatenate(
            [norm_rope(k_blocks[j][:, lo:lo + HEAD_DIM], gk, k_cs[j]) for j in range(3)],
            axis=0).astype(BF16)
        vb = jnp.concatenate([v_blocks[j][:, lo:lo + HEAD_DIM] for j in range(3)],
                             axis=0).astype(BF16)
        qs = []
        sink = jnp.zeros((Q_PER_KV * blk, 1), F32)
        for g in range(Q_PER_KV):
            hq = h * Q_PER_KV + g
            qs.append(norm_rope(q_ref[:, hq * HEAD_DIM:(hq + 1) * HEAD_DIM], gq, cso))
            sink = jnp.where(head_of_row == g, sink_ref[hq], sink)
        qh = jnp.concatenate(qs, axis=0).astype(BF16)
        s = lax.dot_general(qh, kb, (((1,), (1,)), ((), ())),
                            preferred_element_type=F32) * scale
        s = jnp.where(valid, s, NEG_INF)
        m = jnp.maximum(jnp.max(s, axis=-1, keepdims=True), sink)
        p = jnp.exp(s - m)
        denom = jnp.sum(p, axis=-1, keepdims=True) + jnp.exp(sink - m)
        o = jnp.dot(p.astype(BF16), vb, preferred_element_type=F32) / denom
        for g in range(Q_PER_KV):
            hq = h * Q_PER_KV + g
            o_ref[:, hq * HEAD_DIM:(hq + 1) * HEAD_DIM] = o[g * blk:(g + 1) * blk, :]


def _cast_slices(w2d, steps):
    rows, cols = w2d.shape
    rb = -(-rows // steps)
    rb = -(-rb // SUBLANES) * SUBLANES
    n_blocks = -(-rows // rb)
    return rb, cols, n_blocks


def _attention(proj, cs, g_q, g_k, sink, w_gu, w_dn, batch, seq):
    t = proj.shape[0]
    nb = seq // ATTN_BLOCK
    aw = N_Q_HEADS * HEAD_DIM
    kw = N_KV_HEADS * HEAD_DIM
    k_col = aw // kw
    v_col = k_col + 1

    def krow(off, col):
        return lambda b, n: (b * nb + jnp.clip(n + off, 0, nb - 1), col)

    def csrow(off):
        return lambda b, n: (jnp.clip(n + off, 0, nb - 1), 0)

    wgu2 = w_gu.reshape(-1, w_gu.shape[-1])
    wdn2 = w_dn.reshape(-1, w_dn.shape[-1])
    gu_rb, gu_cols, gu_nb = _cast_slices(wgu2, batch * nb)
    dn_rb, dn_cols, dn_nb = _cast_slices(wdn2, batch * nb)
    gu_spec = pl.BlockSpec((gu_rb, gu_cols), lambda b, n: (jnp.minimum(b * nb + n, gu_nb - 1), 0))
    dn_spec = pl.BlockSpec((dn_rb, dn_cols), lambda b, n: (jnp.minimum(b * nb + n, dn_nb - 1), 0))

    attn, wgu_b, wdn_b = pl.pallas_call(
        functools.partial(_attn_body, seq=seq),
        grid=(batch, nb),
        in_specs=[
            pl.BlockSpec(memory_space=pltpu.SMEM),
            pl.BlockSpec((ATTN_BLOCK, aw), krow(0, 0)),
            pl.BlockSpec((ATTN_BLOCK, kw), krow(-1, k_col)),
            pl.BlockSpec((ATTN_BLOCK, kw), krow(0, k_col)),
            pl.BlockSpec((ATTN_BLOCK, kw), krow(1, k_col)),
            pl.BlockSpec((ATTN_BLOCK, kw), krow(-1, v_col)),
            pl.BlockSpec((ATTN_BLOCK, kw), krow(0, v_col)),
            pl.BlockSpec((ATTN_BLOCK, kw), krow(1, v_col)),
            pl.BlockSpec((ATTN_BLOCK, 2 * HEAD_DIM), csrow(0)),
            pl.BlockSpec((ATTN_BLOCK, 2 * HEAD_DIM), csrow(-1)),
            pl.BlockSpec((ATTN_BLOCK, 2 * HEAD_DIM), csrow(1)),
            pl.BlockSpec((1, HEAD_DIM), lambda b, n: (0, 0)),
            pl.BlockSpec((1, HEAD_DIM), lambda b, n: (0, 0)),
            gu_spec,
            dn_spec,
        ],
        out_specs=[
            pl.BlockSpec((ATTN_BLOCK, aw), lambda b, n: (b * nb + n, 0)),
            gu_spec,
            dn_spec,
        ],
        out_shape=[
            jax.ShapeDtypeStruct((t, aw), F32),
            jax.ShapeDtypeStruct(wgu2.shape, BF16),
            jax.ShapeDtypeStruct(wdn2.shape, BF16),
        ],
        compiler_params=pltpu.CompilerParams(
            dimension_semantics=("arbitrary", "arbitrary"), vmem_limit_bytes=VMEM_LIMIT),
        name="attn",
    )(sink, proj, proj, proj, proj, proj, proj, proj, cs, cs, cs, g_q, g_k, wgu2, wdn2)
    return attn, wgu_b.reshape(w_gu.shape), wdn_b.reshape(w_dn.shape)


def _rope_table(seq):
    half = HEAD_DIM // 2
    inv_freq = jnp.float32(ROPE_THETA) ** (-jnp.arange(half, dtype=F32) / half)
    ang = jnp.arange(seq).astype(F32)[:, None] * inv_freq[None, :]
    cos, sin = jnp.cos(ang), jnp.sin(ang)
    return jnp.concatenate([cos, cos, -sin, sin], axis=-1)


_LRU_CHUNK = 256
_HALO = SUBLANES


def _softplus(z):
    return jnp.maximum(z, 0.0) + jnp.log1p(jnp.exp(-jnp.abs(z)))


def _lru_body(x_ref, y_ref, cw_ref, cb_ref, w_ref, bias_ref, lam_ref, o_ref,
              xp_ref, af_ref, uf_ref, ab_ref, ub_ref, hf_ref, hb_ref):
    seq = x_ref.shape[0]
    ch = _LRU_CHUNK
    bw = LRU_BLOCK_W

    zero_halo = jnp.zeros((_HALO, bw), F32)
    xp_ref[0:_HALO, :] = zero_halo
    xp_ref[_HALO + seq:2 * _HALO + seq, :] = zero_halo
    xp_ref[_HALO:_HALO + seq, :] = x_ref[...]

    cw = cw_ref[...]
    cb = cb_ref[...]
    bias = bias_ref[0]
    sp = _softplus(-lam_ref[...])
    a_refs = (af_ref, ab_ref)
    u_refs = (uf_ref, ub_ref)

    def gate_chunk(c, carry):
        base = pl.multiple_of(c * ch, ch)
        win = xp_ref[pl.ds(base, ch + 2 * _HALO), :]
        xc = cb
        for j in range(CONV_WIDTH):
            off = _HALO - 2 + j
            xc = xc + cw[j:j + 1, :] * win[off:off + ch, :]
        gates = jnp.dot(xc.astype(BF16), w_ref[0], preferred_element_type=F32) + bias
        for d in range(2):
            ga = gates[:, (2 * d) * bw:(2 * d + 1) * bw]
            gx = gates[:, (2 * d + 1) * bw:(2 * d + 2) * bw]
            r = jax.nn.sigmoid(ga)
            i = jax.nn.sigmoid(gx)
            log_a = (-LRU_C) * r * sp[d:d + 1, :]
            a = jnp.exp(log_a)
            u = jnp.sqrt(1.0 - a * a) * (i * xc)
            a_refs[d][pl.ds(base, ch), :] = a
            u_refs[d][pl.ds(base, ch), :] = u
        return carry

    lax.fori_loop(0, seq // ch, gate_chunk, 0)

    rowi = lax.broadcasted_iota(I32, (SUBLANES, bw), 0)
    nt = seq // SUBLANES

    def tile_scan(a, u, reverse):
        for d in (1, 2, 4):
            if reverse:
                keep = rowi < SUBLANES - d
                shift = SUBLANES - d
            else:
                keep = rowi >= d
                shift = d
            a_s = jnp.where(keep, pltpu.roll(a, shift, axis=0), 1.0)
            u_s = jnp.where(keep, pltpu.roll(u, shift, axis=0), 0.0)
            u = a * u_s + u
            a = a * a_s
        return a, u

    def bcast_row(x, r):
        return jnp.broadcast_to(x[r:r + 1, :], (SUBLANES, bw))

    def scan_step(t, carry):
        cf, cbk = carry
        rf = pl.multiple_of(t * SUBLANES, SUBLANES)
        a, u = tile_scan(af_ref[pl.ds(rf, SUBLANES), :], uf_ref[pl.ds(rf, SUBLANES), :], False)
        hf_ref[pl.ds(rf, SUBLANES), :] = u + a * cf
        cf = bcast_row(u, SUBLANES - 1) + bcast_row(a, SUBLANES - 1) * cf
        rb = pl.multiple_of((nt - 1 - t) * SUBLANES, SUBLANES)
        a, u = tile_scan(ab_ref[pl.ds(rb, SUBLANES), :], ub_ref[pl.ds(rb, SUBLANES), :], True)
        hb_ref[pl.ds(rb, SUBLANES), :] = u + a * cbk
        cbk = bcast_row(u, 0) + bcast_row(a, 0) * cbk
        return cf, cbk

    zero = jnp.zeros((SUBLANES, bw), F32)
    lax.fori_loop(0, nt, scan_step, (zero, zero), unroll=4)

    def out_chunk(c, carry):
        base = pl.multiple_of(c * ch, ch)
        h = hf_ref[pl.ds(base, ch), :] + hb_ref[pl.ds(base, ch), :]
        o_ref[pl.ds(base, ch), :] = jax.nn.gelu(y_ref[pl.ds(base, ch), :]) * h
        return carry

    lax.fori_loop(0, seq // ch, out_chunk, 0)


def _lru(proj, conv_w, conv_b, w_cat, bias_cat, lam, batch, seq, x_col0, y_col0):
    t = proj.shape[0]
    bw = LRU_BLOCK_W
    nblk = conv_w.shape[1] // bw
    xc0 = x_col0 // bw
    yc0 = y_col0 // bw
    return pl.pallas_call(
        _lru_body,
        grid=(batch, nblk),
        in_specs=[
            pl.BlockSpec((seq, bw), lambda b, c: (b, xc0 + c)),
            pl.BlockSpec((seq, bw), lambda b, c: (b, yc0 + c)),
            pl.BlockSpec((CONV_WIDTH, bw), lambda b, c: (0, c)),
            pl.BlockSpec((1, bw), lambda b, c: (0, c)),
            pl.BlockSpec((1, bw, 4 * bw), lambda b, c: (c, 0, 0)),
            pl.BlockSpec((1, 1, 4 * bw), lambda b, c: (c, 0, 0)),
            pl.BlockSpec((2, bw), lambda b, c: (0, c)),
        ],
        out_specs=pl.BlockSpec((seq, bw), lambda b, c: (b, c)),
        out_shape=jax.ShapeDtypeStruct((t, nblk * bw), F32),
        scratch_shapes=[
            pltpu.VMEM((seq + 2 * _HALO, bw), F32),
            pltpu.VMEM((seq, bw), F32),
            pltpu.VMEM((seq, bw), F32),
            pltpu.VMEM((seq, bw), F32),
            pltpu.VMEM((seq, bw), F32),
            pltpu.VMEM((seq, bw), F32),
            pltpu.VMEM((seq, bw), F32),
        ],
        compiler_params=pltpu.CompilerParams(
            dimension_semantics=("parallel", "parallel"), vmem_limit_bytes=VMEM_LIMIT),
        name="lru",
    )(proj, proj, conv_w, conv_b, w_cat, bias_cat, lam)


def _outproj_body(attn_ref, rec_ref, x_ref, ga_ref, gl_ref, wo_ref, gf_ref, wr_ref, br_ref,
                  x2_ref, h2_ref, pk_ref, cnt_ref, carry_ref, *, n_experts):
    tm = x_ref.shape[0]
    aw = attn_ref.shape[1]

    @pl.when(pl.program_id(0) == 0)
    def _():
        carry_ref[...] = jnp.zeros_like(carry_ref)

    an = _rms(attn_ref[...], ga_ref[...]).astype(BF16)
    rn = _rms(rec_ref[...], gl_ref[...]).astype(BF16)
    y = jnp.dot(an, wo_ref[0:aw, :], preferred_element_type=F32)
    y = y + jnp.dot(rn, wo_ref[aw:, :], preferred_element_type=F32)
    x2 = x_ref[...] + y
    x2_ref[...] = x2
    hn = _rms(x2, gf_ref[...])
    _store_token_slabs(hn, h2_ref)

    hn_hi = hn.astype(BF16)
    hn_lo = (hn - hn_hi.astype(F32)).astype(BF16)
    hi_part = jnp.dot(hn_hi, wr_ref[...], preferred_element_type=F32)
    lo_part = jnp.dot(hn_lo, wr_ref[:, 0:LANES], preferred_element_type=F32)
    logits = hi_part[:, 0:LANES] + (hi_part[:, LANES:] + lo_part) + br_ref[...]
    lane = lax.broadcasted_iota(I32, (tm, LANES), 1)
    lane_f = lane.astype(F32)
    work = jnp.where(lane < n_experts, logits, -jnp.inf)

    sels, tops, hots = [], [], []
    for _ in range(TOP_K):
        m = jnp.max(work, axis=1, keepdims=True)
        sel = jnp.min(jnp.where(work == m, lane_f, float(LANES)), axis=1, keepdims=True)
        hot = lane_f == sel
        work = jnp.where(hot, -jnp.inf, work)
        sels.append(sel)
        tops.append(m)
        hots.append(hot)

    exps = [jnp.exp(tk - tops[0]) for tk in tops]
    den = exps[0]
    for e in exps[1:]:
        den = den + e
    gates = [e / den for e in exps]

    chosen = jnp.zeros((tm, LANES), F32)
    for hot in hots:
        chosen = chosen + jnp.where(hot, 1.0, 0.0)
    ri = lax.broadcasted_iota(I32, (tm, tm), 0)
    ci = lax.broadcasted_iota(I32, (tm, tm), 1)
    tri = jnp.where(ci < ri, 1.0, 0.0).astype(BF16)
    before = jnp.dot(tri, chosen.astype(BF16), preferred_element_type=F32) + carry_ref[0:1, :]
    ranks = [jnp.sum(jnp.where(hot, before, 0.0), axis=1, keepdims=True) for hot in hots]

    total = carry_ref[0:1, :] + jnp.sum(chosen, axis=0, keepdims=True)
    carry_ref[...] = jnp.broadcast_to(total, carry_ref.shape)
    cnt_ref[...] = jnp.broadcast_to(total, cnt_ref.shape)

    pk = jnp.zeros((tm, LANES), F32)
    for k in range(TOP_K):
        pk = jnp.where(lane == _PK_IDX + k, sels[k], pk)
        pk = jnp.where(lane == _PK_RANK + k, ranks[k], pk)
        pk = jnp.where(lane == _PK_GATE + k, gates[k], pk)
    pk_ref[...] = pk


def _outproj(attn, rec, x2d, g_attn, g_lru, w_out, g_ffn, w_router, b_router, n_experts, tm):
    t, d = x2d.shape
    aw = attn.shape[1]
    lw = rec.shape[1]
    const = lambda i: (0, 0)
    return pl.pallas_call(
        functools.partial(_outproj_body, n_experts=n_experts),
        grid=(t // tm,),
        in_specs=[
            pl.BlockSpec((tm, aw), lambda i: (i, 0)),
            pl.BlockSpec((tm, lw), lambda i: (i, 0)),
            pl.BlockSpec((tm, d), lambda i: (i, 0)),
            pl.BlockSpec((1, aw), const),
            pl.BlockSpec((1, lw), const),
            pl.BlockSpec((aw + lw, d), const),
            pl.BlockSpec((1, d), const),
            pl.BlockSpec((d, 2 * LANES), const),
            pl.BlockSpec((1, LANES), const),
        ],
        out_specs=[
            pl.BlockSpec((tm, d), lambda i: (i, 0)),
            pl.BlockSpec((tm * (d // LANES), LANES), lambda i: (i, 0)),
            pl.BlockSpec((tm, LANES), lambda i: (i, 0)),
            pl.BlockSpec((SUBLANES, LANES), const),
        ],
        out_shape=[
            jax.ShapeDtypeStruct((t, d), F32),
            jax.ShapeDtypeStruct((t * (d // LANES), LANES), F32),
            jax.ShapeDtypeStruct((t, LANES), F32),
            jax.ShapeDtypeStruct((SUBLANES, LANES), F32),
        ],
        scratch_shapes=[pltpu.VMEM((SUBLANES, LANES), F32)],
        compiler_params=pltpu.CompilerParams(
            dimension_semantics=("arbitrary",), vmem_limit_bytes=VMEM_LIMIT),
        name="outproj_router",
    )(attn, rec, x2d, g_attn, g_lru, w_out, g_ffn, w_router, b_router)


def _experts_body(te_ref, nu_ref, scur_ref, snxt_ref, dprev_ref, h_ref, wg_ref, wu_ref, bg_ref,
                  bu_ref, wd_ref, bd_ref, y_ref, raw_ref, xb_ref, acc_ref, yst_ref, gsem_ref,
                  ssem_ref, *, n_tok):
    i = pl.program_id(0)
    j = pl.program_id(1)
    _, tm, d = xb_ref.shape
    nc = d // LANES
    nu = nu_ref[0]
    slot = lax.rem(i, 2)

    def gather_start(s_ref):
        def body(r, carry):
            _slab_copy(h_ref, s_ref[0, 0, r], raw_ref, r, nc, gsem_ref.at[0]).start()
            return carry

        lax.fori_loop(0, tm, body, 0, unroll=8)

    def gather_wait():
        pltpu.make_async_copy(h_ref.at[pl.ds(0, tm * nc), :], raw_ref, gsem_ref.at[0]).wait()

    def scatter_start():
        def body(r, carry):
            _slab_copy(yst_ref, r, y_ref, dprev_ref[0, 0, r], nc, ssem_ref.at[0]).start()
            return carry

        lax.fori_loop(0, tm, body, 0, unroll=8)

    def scatter_wait():
        pltpu.make_async_copy(yst_ref, y_ref.at[pl.ds(0, tm * nc), :], ssem_ref.at[0]).wait()

    def slabs_to_lhs(s):
        for c in range(nc):
            xb_ref[s, :, c * LANES:(c + 1) * LANES] = _load_token_chunk(
                raw_ref, 0, tm, nc, c).astype(BF16)

    def ffn_half():
        x = xb_ref[slot]
        gate = jnp.dot(x, wg_ref[...], preferred_element_type=F32) + bg_ref[0]
        up = jnp.dot(x, wu_ref[...], preferred_element_type=F32) + bu_ref[0]
        gate = jnp.minimum(gate, SWIGLU_LIMIT)
        up = jnp.clip(up, -SWIGLU_LIMIT, SWIGLU_LIMIT)
        hidden = (up + 1.0) * (gate * jax.nn.sigmoid(SWIGLU_ALPHA * gate))
        return jnp.dot(hidden.astype(BF16), wd_ref[...], preferred_element_type=F32)

    @pl.when((i == 0) & (j == 0))
    def _():
        acc_ref[1] = jnp.zeros((tm, d), F32)
        gather_start(scur_ref)
        gather_wait()
        slabs_to_lhs(0)

    @pl.when(i < nu)
    def _():
        @pl.when(j == 0)
        def _():
            @pl.when(i >= 1)
            def _():
                scatter_wait()

            for r in range(tm):
                _slab_copy(h_ref, snxt_ref[0, 0, r], raw_ref, r, nc, gsem_ref.at[0]).start()
            acc_ref[slot] = bd_ref[0] + ffn_half()
            _store_token_slabs(acc_ref[1 - slot], yst_ref)

        @pl.when(j == 1)
        def _():
            gather_wait()
            for r in range(tm):
                dst = jnp.where(i >= 1, dprev_ref[0, 0, r], n_tok * TOP_K + r)
                _slab_copy(yst_ref, r, y_ref, dst, nc, ssem_ref.at[0]).start()
            acc_ref[slot] += ffn_half()
            slabs_to_lhs(1 - slot)

    @pl.when((i == nu) & (j == 0) & (nu > 0))
    def _():
        scatter_wait()
        _store_token_slabs(acc_ref[1 - slot], yst_ref)
        scatter_start()
        scatter_wait()


def _experts(tile_expert, n_used, row_asg, h_slabs, w_gu, b_gu, w_dn, b_dn, tm, fc):
    n_exp, d, two_f = w_gu.shape
    nc = d // LANES
    n_tok = h_slabs.shape[0] // nc
    f = two_f // 2
    nf = f // fc
    n_tiles = row_asg.shape[0] // tm
    pad = row_asg >= n_tok * TOP_K
    row_tok = row_asg // TOP_K
    src3 = jnp.where(pad, 0, row_tok).reshape(n_tiles, 1, tm)
    spare = n_tok * TOP_K + jnp.arange(n_tiles * tm, dtype=I32) % tm
    dst3 = jnp.where(pad, spare, (row_asg % TOP_K) * n_tok + row_tok).reshape(n_tiles, 1, tm)

    assert nf == 2, "the expert kernel pipelines row traffic over exactly two hidden-dim halves"

    def tile(i, nu):
        return jnp.clip(i, 0, jnp.maximum(nu[0] - 1, 0))

    def chunk(i, j, nu):
        return jnp.where(i < nu[0], j, nf - 1)

    grid_spec = pltpu.PrefetchScalarGridSpec(
        num_scalar_prefetch=2,
        grid=(n_tiles, nf),
        in_specs=[
            pl.BlockSpec((1, 1, tm), lambda i, j, te, nu: (tile(i, nu), 0, 0),
                         memory_space=pltpu.SMEM),
            pl.BlockSpec((1, 1, tm), lambda i, j, te, nu: (tile(i + 1, nu), 0, 0),
                         memory_space=pltpu.SMEM),
            pl.BlockSpec((1, 1, tm), lambda i, j, te, nu: (tile(i - 1, nu), 0, 0),
                         memory_space=pltpu.SMEM),
            pl.BlockSpec(memory_space=pl.ANY),
            pl.BlockSpec((None, d, fc), lambda i, j, te, nu: (te[tile(i, nu)], 0, chunk(i, j, nu))),
            pl.BlockSpec((None, d, fc), lambda i, j, te, nu: (te[tile(i, nu)], 0, nf + chunk(i, j, nu))),
            pl.BlockSpec((None, 1, fc), lambda i, j, te, nu: (te[tile(i, nu)], 0, chunk(i, j, nu))),
            pl.BlockSpec((None, 1, fc), lambda i, j, te, nu: (te[tile(i, nu)], 0, nf + chunk(i, j, nu))),
            pl.BlockSpec((None, fc, d), lambda i, j, te, nu: (te[tile(i, nu)], chunk(i, j, nu), 0)),
            pl.BlockSpec((None, 1, d), lambda i, j, te, nu: (te[tile(i, nu)], 0, 0)),
        ],
        out_specs=pl.BlockSpec(memory_space=pl.ANY),
        scratch_shapes=[
            pltpu.VMEM((tm * nc, LANES), F32),
            pltpu.VMEM((2, tm, d), BF16),
            pltpu.VMEM((2, tm, d), F32),
            pltpu.VMEM((tm * nc, LANES), F32),
            pltpu.SemaphoreType.DMA((1,)),
            pltpu.SemaphoreType.DMA((1,)),
        ],
    )
    return pl.pallas_call(
        functools.partial(_experts_body, n_tok=n_tok),
        grid_spec=grid_spec,
        out_shape=jax.ShapeDtypeStruct(((n_tok * TOP_K + tm) * nc, LANES), F32),
        compiler_params=pltpu.CompilerParams(
            dimension_semantics=("arbitrary", "arbitrary"),
            vmem_limit_bytes=EXPERTS_VMEM_LIMIT),
        name="experts",
    )(tile_expert, n_used, src3, src3, dst3, h_slabs, w_gu, w_gu, b_gu, b_gu, w_dn, b_dn)


def _combine_body(x2_ref, pk_ref, *rest):
    y_refs, o_ref = rest[:TOP_K], rest[TOP_K]
    tm, d = x2_ref.shape
    nc = d // LANES
    pk = pk_ref[...]
    gates = [jnp.broadcast_to(pk[:, _PK_GATE + k:_PK_GATE + k + 1], (tm, LANES))
             for k in range(TOP_K)]
    for c in range(nc):
        acc = x2_ref[:, c * LANES:(c + 1) * LANES]
        for k in range(TOP_K):
            acc = acc + gates[k] * _load_token_chunk(y_refs[k], 0, tm, nc, c)
        o_ref[:, c * LANES:(c + 1) * LANES] = acc


def _combine(x2, pk, y, tm):
    t, d = x2.shape
    nc = d // LANES
    steps = t // tm

    def y_spec(k):
        return pl.BlockSpec((tm * nc, LANES), lambda i: (k * steps + i, 0))

    return pl.pallas_call(
        _combine_body,
        grid=(steps,),
        in_specs=[
            pl.BlockSpec((tm, d), lambda i: (i, 0)),
            pl.BlockSpec((tm, LANES), lambda i: (i, 0)),
        ] + [y_spec(k) for k in range(TOP_K)],
        out_specs=pl.BlockSpec((tm, d), lambda i: (i, 0)),
        out_shape=jax.ShapeDtypeStruct((t, d), F32),
        compiler_params=pltpu.CompilerParams(
            dimension_semantics=("parallel",), vmem_limit_bytes=VMEM_LIMIT),
        name="combine",
    )(x2, pk, *([y] * TOP_K))


def _tile_sizes(t):
    def pick(pref):
        while t % pref:
            pref //= 2
        return pref
    return dict(inproj=pick(1024), outproj=pick(512), expert=512, ffn_chunk=1024,
                combine=pick(256))


def _layer(x, g_norm_mix, w_in, b_in, g_q, g_k, sink, conv_w, conv_b, w_rg_a, b_rg_a,
           w_rg_x, b_rg_x, lam, g_attn_out, g_lru_out, w_out, g_norm_ffn, w_router, b_router,
           w_gate_up, b_gate_up, w_down, b_down):
    batch, seq, d = x.shape
    t = batch * seq
    ts = _tile_sizes(t)
    x2d = x.reshape(t, d)
    aw = N_Q_HEADS * HEAD_DIM
    kw = N_KV_HEADS * HEAD_DIM
    lw = conv_w.shape[1]
    nblk = lw // LRU_BLOCK_W
    n_exp = w_router.shape[1]

    proj = _inproj(x2d, g_norm_mix[None, :], w_in.astype(BF16), b_in[None, :], ts["inproj"], 512)

    attn, w_gu_bf16, w_dn_bf16 = _attention(proj, _rope_table(seq), g_q[None, :], g_k[None, :],
                                            sink, w_gate_up, w_down, batch, seq)

    w_cat = jnp.concatenate([w_rg_a[0], w_rg_x[0], w_rg_a[1], w_rg_x[1]], axis=-1).astype(BF16)
    bias_cat = jnp.concatenate(
        [b.reshape(nblk, 1, LRU_BLOCK_W) for b in (b_rg_a[0], b_rg_x[0], b_rg_a[1], b_rg_x[1])],
        axis=-1)
    rec = _lru(proj, conv_w, conv_b[None, :], w_cat, bias_cat, lam, batch, seq,
               x_col0=aw + 2 * kw, y_col0=aw + 2 * kw + lw)

    w_r = jnp.zeros((d, LANES), F32).at[:, :n_exp].set(w_router)
    w_r_hi = w_r.astype(BF16)
    w_r = jnp.concatenate([w_r_hi, (w_r - w_r_hi.astype(F32)).astype(BF16)], axis=1)
    b_r = jnp.zeros((1, LANES), F32).at[0, :n_exp].set(b_router)
    x2, h2, pk, cnt = _outproj(attn, rec, x2d, g_attn_out[None, :], g_lru_out[None, :],
                               w_out.astype(BF16), g_norm_ffn[None, :], w_r, b_r, n_exp,
                               ts["outproj"])

    tme = ts["expert"]
    idx = pk[:, _PK_IDX:_PK_IDX + TOP_K].astype(I32)
    rank = pk[:, _PK_RANK:_PK_RANK + TOP_K].astype(I32)
    counts = cnt[0, :n_exp].astype(I32)
    padded = (counts + tme - 1) // tme * tme
    pend = jnp.cumsum(padded)
    pstart = pend - padded
    dest = (pstart[idx] + rank).reshape(t * TOP_K)
    n_tiles = -(-(t * TOP_K) // tme) + n_exp
    tile_start = jnp.arange(n_tiles, dtype=I32) * tme
    tile_expert = jnp.minimum(
        jnp.sum((pend[None, :] <= tile_start[:, None]).astype(I32), axis=1), n_exp - 1)
    n_used = (pend[-1:] // tme).astype(I32)
    row_asg = jnp.full((n_tiles * tme,), t * TOP_K, I32).at[dest].set(
        jnp.arange(t * TOP_K, dtype=I32), unique_indices=True)

    f = w_gate_up.shape[-1] // 2
    y = _experts(tile_expert, n_used, row_asg, h2, w_gu_bf16,
                 b_gate_up.reshape(n_exp, 1, 2 * f), w_dn_bf16,
                 b_down.reshape(n_exp, 1, d), tme, ts["ffn_chunk"])
    out = _combine(x2, pk, y, ts["combine"])
    return out.reshape(batch, seq, d)


def kernel(x, g_norm_mix, w_in, b_in, g_q, g_k, sink, conv_w, conv_b, w_rg_a, b_rg_a, w_rg_x, b_rg_x, lam, g_attn_out, g_lru_out, w_out, g_norm_ffn, w_router, b_router, w_gate_up, b_gate_up, w_down, b_down):
    depth = w_in.shape[0]
    for l in range(depth):
        x = _layer(x, g_norm_mix[l], w_in[l], b_in[l], g_q[l], g_k[l], sink[l], conv_w[l],
                   conv_b[l], w_rg_a[l], b_rg_a[l], w_rg_x[l], b_rg_x[l], lam[l], g_attn_out[l],
                   g_lru_out[l], w_out[l], g_norm_ffn[l], w_router[l], b_router[l],
                   w_gate_up[l], b_gate_up[l], w_down[l], b_down[l])
    return x
```

```python
import functools

import jax
import jax.numpy as jnp
from jax import lax
from jax.experimental import pallas as pl
from jax.experimental.pallas import tpu as pltpu

F32 = jnp.float32
BF16 = jnp.bfloat16
I32 = jnp.int32

NORM_EPS = 1e-6
HEAD_DIM = 128
N_Q_HEADS = 8
N_KV_HEADS = 2
Q_PER_KV = N_Q_HEADS // N_KV_HEADS
ATTN_BLOCK = 128
WINDOW = 128
ROPE_THETA = 10000.0
NEG_INF = -1e30
LRU_C = 8.0
LRU_BLOCK_W = 128
CONV_WIDTH = 4
TOP_K = 4
SWIGLU_ALPHA = 1.702
SWIGLU_LIMIT = 7.0

LANES = 128
SUBLANES = 8
VMEM_LIMIT = 56 * 1024 * 1024
EXPERTS_VMEM_LIMIT = 62 * 1024 * 1024
ROW_DMA_PRIORITY = 1

_PK_IDX, _PK_RANK, _PK_GATE = 0, TOP_K, 2 * TOP_K


def _rms(x, g):
    ms = jnp.mean(x * x, axis=-1, keepdims=True)
    return x * lax.rsqrt(ms + NORM_EPS) * g


def _store_token_slabs(val, ref, row0=0):
    rows, d = val.shape
    nc = d // LANES
    for c in range(nc):
        ref[pl.ds(row0 * nc + c, rows, stride=nc), :] = val[:, c * LANES:(c + 1) * LANES]


def _load_token_chunk(ref, start_row, rows, nc, c):
    return ref[pl.ds(start_row + c, rows, stride=nc), :]


def _slab_copy(src_ref, src_tok, dst_ref, dst_tok, nc, sem):
    return pltpu.make_async_copy(
        src_ref.at[pl.ds(pl.multiple_of(src_tok * nc, nc), nc), :],
        dst_ref.at[pl.ds(pl.multiple_of(dst_tok * nc, nc), nc), :], sem)


def _inproj_body(x_ref, g_ref, w_ref, b_ref, o_ref, hn_ref):
    @pl.when(pl.program_id(1) == 0)
    def _():
        hn_ref[...] = _rms(x_ref[...], g_ref[...]).astype(BF16)

    o_ref[...] = jnp.dot(hn_ref[...], w_ref[...], preferred_element_type=F32) + b_ref[...]


def _inproj(x2d, g, w, b, tm, tn):
    t, d = x2d.shape
    n = w.shape[1]
    return pl.pallas_call(
        _inproj_body,
        grid=(t // tm, n // tn),
        in_specs=[
            pl.BlockSpec((tm, d), lambda i, j: (i, 0)),
            pl.BlockSpec((1, d), lambda i, j: (0, 0)),
            pl.BlockSpec((d, tn), lambda i, j: (0, j)),
            pl.BlockSpec((1, tn), lambda i, j: (0, j)),
        ],
        out_specs=pl.BlockSpec((tm, tn), lambda i, j: (i, j)),
        out_shape=jax.ShapeDtypeStruct((t, n), F32),
        scratch_shapes=[pltpu.VMEM((tm, d), BF16)],
        compiler_params=pltpu.CompilerParams(
            dimension_semantics=("parallel", "arbitrary"), vmem_limit_bytes=VMEM_LIMIT),
        name="inproj",
    )(x2d, g, w, b)


def _attn_body(sink_ref, q_ref, kp_ref, ko_ref, kn_ref, vp_ref, vo_ref, vn_ref,
               cso_ref, csp_ref, csn_ref, gq_ref, gk_ref, wgu_ref, wdn_ref,
               o_ref, wgu_o_ref, wdn_o_ref, *, seq):
    n = pl.program_id(1)
    blk = ATTN_BLOCK

    wgu_o_ref[...] = wgu_ref[...].astype(BF16)
    wdn_o_ref[...] = wdn_ref[...].astype(BF16)

    def norm_rope(x, g, cs):
        y = _rms(x, g)
        return y * cs[:, :HEAD_DIM] + pltpu.roll(y, HEAD_DIM // 2, axis=1) * cs[:, HEAD_DIM:]

    gq = gq_ref[...]
    gk = gk_ref[...]
    cso = cso_ref[...]
    k_cs = (csp_ref[...], cso, csn_ref[...])
    k_blocks = (kp_ref, ko_ref, kn_ref)
    v_blocks = (vp_ref, vo_ref, vn_ref)

    rows = lax.broadcasted_iota(I32, (Q_PER_KV * blk, 3 * blk), 0)
    cols = lax.broadcasted_iota(I32, (Q_PER_KV * blk, 3 * blk), 1)
    qpos = n * blk + (rows & (blk - 1))
    kpos = (n - 1) * blk + cols
    valid = (jnp.abs(qpos - kpos) <= WINDOW) & (kpos >= 0) & (kpos < seq)
    head_of_row = lax.shift_right_logical(
        lax.broadcasted_iota(I32, (Q_PER_KV * blk, 1), 0), blk.bit_length() - 1)

    scale = HEAD_DIM ** -0.5
    for h in range(N_KV_HEADS):
        lo = h * HEAD_DIM
        kb = jnp.concatenate(
            [norm_rope(k_blocks[j][:, lo:lo + HEAD_DIM], gk, k_cs[j]) for j in range(3)],
            axis=0).astype(BF16)
        vb = jnp.concatenate([v_blocks[j][:, lo:lo + HEAD_DIM] for j in range(3)],
                             axis=0).astype(BF16)
        qs = []
        sink = jnp.zeros((Q_PER_KV * blk, 1), F32)
        for g in range(Q_PER_KV):
            hq = h * Q_PER_KV + g
            qs.append(norm_rope(q_ref[:, hq * HEAD_DIM:(hq + 1) * HEAD_DIM], gq, cso))
            sink = jnp.where(head_of_row == g, sink_ref[hq], sink)
        qh = jnp.concatenate(qs, axis=0).astype(BF16)
        s = lax.dot_general(qh, kb, (((1,), (1,)), ((), ())),
                            preferred_element_type=F32) * scale
        s = jnp.where(valid, s, NEG_INF)
        m = jnp.maximum(jnp.max(s, axis=-1, keepdims=True), sink)
        p = jnp.exp(s - m)
        denom = jnp.sum(p, axis=-1, keepdims=True) + jnp.exp(sink - m)
        o = jnp.dot(p.astype(BF16), vb, preferred_element_type=F32) / denom
        for g in range(Q_PER_KV):
            hq = h * Q_PER_KV + g
            o_ref[:, hq * HEAD_DIM:(hq + 1) * HEAD_DIM] = o[g * blk:(g + 1) * blk, :]


def _cast_slices(w2d, steps):
    rows, cols = w2d.shape
    rb = -(-rows // steps)
    rb = -(-rb // SUBLANES) * SUBLANES
    n_blocks = -(-rows // rb)
    return rb, cols, n_blocks


def _attention(proj, cs, g_q, g_k, sink, w_gu, w_dn, batch, seq):
    t = proj.shape[0]
    nb = seq // ATTN_BLOCK
    aw = N_Q_HEADS * HEAD_DIM
    kw = N_KV_HEADS * HEAD_DIM
    k_col = aw // kw
    v_col = k_col + 1

    def krow(off, col):
        return lambda b, n: (b * nb + jnp.clip(n + off, 0, nb - 1), col)

    def csrow(off):
        return lambda b, n: (jnp.clip(n + off, 0, nb - 1), 0)

    wgu2 = w_gu.reshape(-1, w_gu.shape[-1])
    wdn2 = w_dn.reshape(-1, w_dn.shape[-1])
    gu_rb, gu_cols, gu_nb = _cast_slices(wgu2, batch * nb)
    dn_rb, dn_cols, dn_nb = _cast_slices(wdn2, batch * nb)
    gu_spec = pl.BlockSpec((gu_rb, gu_cols), lambda b, n: (jnp.minimum(b * nb + n, gu_nb - 1), 0))
    dn_spec = pl.BlockSpec((dn_rb, dn_cols), lambda b, n: (jnp.minimum(b * nb + n, dn_nb - 1), 0))

    attn, wgu_b, wdn_b = pl.pallas_call(
        functools.partial(_attn_body, seq=seq),
        grid=(batch, nb),
        in_specs=[
            pl.BlockSpec(memory_space=pltpu.SMEM),
            pl.BlockSpec((ATTN_BLOCK, aw), krow(0, 0)),
            pl.BlockSpec((ATTN_BLOCK, kw), krow(-1, k_col)),
            pl.BlockSpec((ATTN_BLOCK, kw), krow(0, k_col)),
            pl.BlockSpec((ATTN_BLOCK, kw), krow(1, k_col)),
            pl.BlockSpec((ATTN_BLOCK, kw), krow(-1, v_col)),
            pl.BlockSpec((ATTN_BLOCK, kw), krow(0, v_col)),
            pl.BlockSpec((ATTN_BLOCK, kw), krow(1, v_col)),
            pl.BlockSpec((ATTN_BLOCK, 2 * HEAD_DIM), csrow(0)),
            pl.BlockSpec((ATTN_BLOCK, 2 * HEAD_DIM), csrow(-1)),
            pl.BlockSpec((ATTN_BLOCK, 2 * HEAD_DIM), csrow(1)),
            pl.BlockSpec((1, HEAD_DIM), lambda b, n: (0, 0)),
            pl.BlockSpec((1, HEAD_DIM), lambda b, n: (0, 0)),
            gu_spec,
            dn_spec,
        ],
        out_specs=[
            pl.BlockSpec((ATTN_BLOCK, aw), lambda b, n: (b * nb + n, 0)),
            gu_spec,
            dn_spec,
        ],
        out_shape=[
            jax.ShapeDtypeStruct((t, aw), F32),
            jax.ShapeDtypeStruct(wgu2.shape, BF16),
            jax.ShapeDtypeStruct(wdn2.shape, BF16),
        ],
        compiler_params=pltpu.CompilerParams(
            dimension_semantics=("arbitrary", "arbitrary"), vmem_limit_bytes=VMEM_LIMIT),
        name="attn",
    )(sink, proj, proj, proj, proj, proj, proj, proj, cs, cs, cs, g_q, g_k, wgu2, wdn2)
    return attn, wgu_b.reshape(w_gu.shape), wdn_b.reshape(w_dn.shape)


def _rope_table(seq):
    half = HEAD_DIM // 2
    inv_freq = jnp.float32(ROPE_THETA) ** (-jnp.arange(half, dtype=F32) / half)
    ang = jnp.arange(seq).astype(F32)[:, None] * inv_freq[None, :]
    cos, sin = jnp.cos(ang), jnp.sin(ang)
    return jnp.concatenate([cos, cos, -sin, sin], axis=-1)


_LRU_CHUNK = 256
_HALO = SUBLANES


def _softplus(z):
    return jnp.maximum(z, 0.0) + jnp.log1p(jnp.exp(-jnp.abs(z)))


def _lru_body(x_ref, y_ref, cw_ref, cb_ref, w_ref, bias_ref, lam_ref, o_ref,
              xp_ref, af_ref, uf_ref, ab_ref, ub_ref, hf_ref, hb_ref):
    seq = x_ref.shape[0]
    ch = _LRU_CHUNK
    bw = LRU_BLOCK_W

    zero_halo = jnp.zeros((_HALO, bw), F32)
    xp_ref[0:_HALO, :] = zero_halo
    xp_ref[_HALO + seq:2 * _HALO + seq, :] = zero_halo
    xp_ref[_HALO:_HALO + seq, :] = x_ref[...]

    cw = cw_ref[...]
    cb = cb_ref[...]
    bias = bias_ref[0]
    sp = _softplus(-lam_ref[...])
    a_refs = (af_ref, ab_ref)
    u_refs = (uf_ref, ub_ref)

    def gate_chunk(c, carry):
        base = pl.multiple_of(c * ch, ch)
        win = xp_ref[pl.ds(base, ch + 2 * _HALO), :]
        xc = cb
        for j in range(CONV_WIDTH):
            off = _HALO - 2 + j
            xc = xc + cw[j:j + 1, :] * win[off:off + ch, :]
        gates = jnp.dot(xc.astype(BF16), w_ref[0], preferred_element_type=F32) + bias
        for d in range(2):
            ga = gates[:, (2 * d) * bw:(2 * d + 1) * bw]
            gx = gates[:, (2 * d + 1) * bw:(2 * d + 2) * bw]
            r = jax.nn.sigmoid(ga)
            i = jax.nn.sigmoid(gx)
            log_a = (-LRU_C) * r * sp[d:d + 1, :]
            a = jnp.exp(log_a)
            u = jnp.sqrt(1.0 - a * a) * (i * xc)
            a_refs[d][pl.ds(base, ch), :] = a
            u_refs[d][pl.ds(base, ch), :] = u
        return carry

    lax.fori_loop(0, seq // ch, gate_chunk, 0)

    rowi = lax.broadcasted_iota(I32, (SUBLANES, bw), 0)
    nt = seq // SUBLANES

    def tile_scan(a, u, reverse):
        for d in (1, 2, 4):
            if reverse:
                keep = rowi < SUBLANES - d
                shift = SUBLANES - d
            else:
                keep = rowi >= d
                shift = d
            a_s = jnp.where(keep, pltpu.roll(a, shift, axis=0), 1.0)
            u_s = jnp.where(keep, pltpu.roll(u, shift, axis=0), 0.0)
            u = a * u_s + u
            a = a * a_s
        return a, u

    def bcast_row(x, r):
        return jnp.broadcast_to(x[r:r + 1, :], (SUBLANES, bw))

    def scan_step(t, carry):
        cf, cbk = carry
        rf = pl.multiple_of(t * SUBLANES, SUBLANES)
        a, u = tile_scan(af_ref[pl.ds(rf, SUBLANES), :], uf_ref[pl.ds(rf, SUBLANES), :], False)
        hf_ref[pl.ds(rf, SUBLANES), :] = u + a * cf
        cf = bcast_row(u, SUBLANES - 1) + bcast_row(a, SUBLANES - 1) * cf
        rb = pl.multiple_of((nt - 1 - t) * SUBLANES, SUBLANES)
        a, u = tile_scan(ab_ref[pl.ds(rb, SUBLANES), :], ub_ref[pl.ds(rb, SUBLANES), :], True)
        hb_ref[pl.ds(rb, SUBLANES), :] = u + a * cbk
        cbk = bcast_row(u, 0) + bcast_row(a, 0) * cbk
        return cf, cbk

    zero = jnp.zeros((SUBLANES, bw), F32)
    lax.fori_loop(0, nt, scan_step, (zero, zero), unroll=4)

    def out_chunk(c, carry):
        base = pl.multiple_of(c * ch, ch)
        h = hf_ref[pl.ds(base, ch), :] + hb_ref[pl.ds(base, ch), :]
        o_ref[pl.ds(base, ch), :] = jax.nn.gelu(y_ref[pl.ds(base, ch), :]) * h
        return carry

    lax.fori_loop(0, seq // ch, out_chunk, 0)


def _lru(proj, conv_w, conv_b, w_cat, bias_cat, lam, batch, seq, x_col0, y_col0):
    t = proj.shape[0]
    bw = LRU_BLOCK_W
    nblk = conv_w.shape[1] // bw
    xc0 = x_col0 // bw
    yc0 = y_col0 // bw
    return pl.pallas_call(
        _lru_body,
        grid=(batch, nblk),
        in_specs=[
            pl.BlockSpec((seq, bw), lambda b, c: (b, xc0 + c)),
            pl.BlockSpec((seq, bw), lambda b, c: (b, yc0 + c)),
            pl.BlockSpec((CONV_WIDTH, bw), lambda b, c: (0, c)),
            pl.BlockSpec((1, bw), lambda b, c: (0, c)),
            pl.BlockSpec((1, bw, 4 * bw), lambda b, c: (c, 0, 0)),
            pl.BlockSpec((1, 1, 4 * bw), lambda b, c: (c, 0, 0)),
            pl.BlockSpec((2, bw), lambda b, c: (0, c)),
        ],
        out_specs=pl.BlockSpec((seq, bw), lambda b, c: (b, c)),
        out_shape=jax.ShapeDtypeStruct((t, nblk * bw), F32),
        scratch_shapes=[
            pltpu.VMEM((seq + 2 * _HALO, bw), F32),
            pltpu.VMEM((seq, bw), F32),
            pltpu.VMEM((seq, bw), F32),
            pltpu.VMEM((seq, bw), F32),
            pltpu.VMEM((seq, bw), F32),
            pltpu.VMEM((seq, bw), F32),
            pltpu.VMEM((seq, bw), F32),
        ],
        compiler_params=pltpu.CompilerParams(
            dimension_semantics=("parallel", "parallel"), vmem_limit_bytes=VMEM_LIMIT),
        name="lru",
    )(proj, proj, conv_w, conv_b, w_cat, bias_cat, lam)


def _outproj_body(attn_ref, rec_ref, x_ref, ga_ref, gl_ref, wo_ref, gf_ref, wr_ref, br_ref,
                  x2_ref, h2_ref, pk_ref, cnt_ref, carry_ref, *, n_experts):
    tm = x_ref.shape[0]
    aw = attn_ref.shape[1]

    @pl.when(pl.program_id(0) == 0)
    def _():
        carry_ref[...] = jnp.zeros_like(carry_ref)

    an = _rms(attn_ref[...], ga_ref[...]).astype(BF16)
    rn = _rms(rec_ref[...], gl_ref[...]).astype(BF16)
    y = jnp.dot(an, wo_ref[0:aw, :], preferred_element_type=F32)
    y = y + jnp.dot(rn, wo_ref[aw:, :], preferred_element_type=F32)
    x2 = x_ref[...] + y
    x2_ref[...] = x2
    hn = _rms(x2, gf_ref[...])
    _store_token_slabs(hn, h2_ref)

    hn_hi = hn.astype(BF16)
    hn_lo = (hn - hn_hi.astype(F32)).astype(BF16)
    hi_part = jnp.dot(hn_hi, wr_ref[...], preferred_element_type=F32)
    lo_part = jnp.dot(hn_lo, wr_ref[:, 0:LANES], preferred_element_type=F32)
    logits = hi_part[:, 0:LANES] + (hi_part[:, LANES:] + lo_part) + br_ref[...]
    lane = lax.broadcasted_iota(I32, (tm, LANES), 1)
    lane_f = lane.astype(F32)
    work = jnp.where(lane < n_experts, logits, -jnp.inf)

    sels, tops, hots = [], [], []
    for _ in range(TOP_K):
        m = jnp.max(work, axis=1, keepdims=True)
        sel = jnp.min(jnp.where(work == m, lane_f, float(LANES)), axis=1, keepdims=True)
        hot = lane_f == sel
        work = jnp.where(hot, -jnp.inf, work)
        sels.append(sel)
        tops.append(m)
        hots.append(hot)

    exps = [jnp.exp(tk - tops[0]) for tk in tops]
    den = exps[0]
    for e in exps[1:]:
        den = den + e
    gates = [e / den for e in exps]

    chosen = jnp.zeros((tm, LANES), F32)
    for hot in hots:
        chosen = chosen + jnp.where(hot, 1.0, 0.0)
    ri = lax.broadcasted_iota(I32, (tm, tm), 0)
    ci = lax.broadcasted_iota(I32, (tm, tm), 1)
    tri = jnp.where(ci < ri, 1.0, 0.0).astype(BF16)
    before = jnp.dot(tri, chosen.astype(BF16), preferred_element_type=F32) + carry_ref[0:1, :]
    ranks = [jnp.sum(jnp.where(hot, before, 0.0), axis=1, keepdims=True) for hot in hots]

    total = carry_ref[0:1, :] + jnp.sum(chosen, axis=0, keepdims=True)
    carry_ref[...] = jnp.broadcast_to(total, carry_ref.shape)
    cnt_ref[...] = jnp.broadcast_to(total, cnt_ref.shape)

    pk = jnp.zeros((tm, LANES), F32)
    for k in range(TOP_K):
        pk = jnp.where(lane == _PK_IDX + k, sels[k], pk)
        pk = jnp.where(lane == _PK_RANK + k, ranks[k], pk)
        pk = jnp.where(lane == _PK_GATE + k, gates[k], pk)
    pk_ref[...] = pk


def _outproj(attn, rec, x2d, g_attn, g_lru, w_out, g_ffn, w_router, b_router, n_experts, tm):
    t, d = x2d.shape
    aw = attn.shape[1]
    lw = rec.shape[1]
    const = lambda i: (0, 0)
    return pl.pallas_call(
        functools.partial(_outproj_body, n_experts=n_experts),
        grid=(t // tm,),
        in_specs=[
            pl.BlockSpec((tm, aw), lambda i: (i, 0)),
            pl.BlockSpec((tm, lw), lambda i: (i, 0)),
            pl.BlockSpec((tm, d), lambda i: (i, 0)),
            pl.BlockSpec((1, aw), const),
            pl.BlockSpec((1, lw), const),
            pl.BlockSpec((aw + lw, d), const),
            pl.BlockSpec((1, d), const),
            pl.BlockSpec((d, 2 * LANES), const),
            pl.BlockSpec((1, LANES), const),
        ],
        out_specs=[
            pl.BlockSpec((tm, d), lambda i: (i, 0)),
            pl.BlockSpec((tm * (d // LANES), LANES), lambda i: (i, 0)),
            pl.BlockSpec((tm, LANES), lambda i: (i, 0)),
            pl.BlockSpec((SUBLANES, LANES), const),
        ],
        out_shape=[
            jax.ShapeDtypeStruct((t, d), F32),
            jax.ShapeDtypeStruct((t * (d // LANES), LANES), F32),
            jax.ShapeDtypeStruct((t, LANES), F32),
            jax.ShapeDtypeStruct((SUBLANES, LANES), F32),
        ],
        scratch_shapes=[pltpu.VMEM((SUBLANES, LANES), F32)],
        compiler_params=pltpu.CompilerParams(
            dimension_semantics=("arbitrary",), vmem_limit_bytes=VMEM_LIMIT),
        name="outproj_router",
    )(attn, rec, x2d, g_attn, g_lru, w_out, g_ffn, w_router, b_router)


def _experts_body(te_ref, nu_ref, s0_ref, s1_ref, s2_ref, dprev_ref, h_ref, wg_ref, wu_ref,
                  bg_ref, bu_ref, wd_ref, bd_ref, y_ref, raw_ref, xb_ref, acc_ref, yst_ref,
                  gsem_ref, ssem_ref, *, n_tok):
    i = pl.program_id(0)
    j = pl.program_id(1)
    _, tm, d = xb_ref.shape
    nc = d // LANES
    rows = tm * nc
    nu = nu_ref[0]
    slot = lax.rem(i, 2)
    other = 1 - slot

    def buf_row(b):
        return pl.multiple_of(b * rows, rows)

    def gather_one(s_ref, b, r):
        _slab_copy(h_ref, s_ref[0, 0, r], raw_ref, b * tm + r, nc,
                   gsem_ref.at[b]).start(priority=ROW_DMA_PRIORITY)

    def gather_loop(s_ref, b):
        def body(r, carry):
            gather_one(s_ref, b, r)
            return carry

        lax.fori_loop(0, tm, body, 0, unroll=8)

    def gather_wait(b):
        pltpu.make_async_copy(h_ref.at[pl.ds(0, rows), :], raw_ref.at[pl.ds(buf_row(b), rows), :],
                              gsem_ref.at[b]).wait()

    def scatter_one(b, r, dst):
        _slab_copy(yst_ref, b * tm + r, y_ref, dst, nc,
                   ssem_ref.at[b]).start(priority=ROW_DMA_PRIORITY)

    def scatter_wait(b):
        pltpu.make_async_copy(yst_ref.at[pl.ds(buf_row(b), rows), :], y_ref.at[pl.ds(0, rows), :],
                              ssem_ref.at[b]).wait()

    def slabs_to_lhs(b):
        for c in range(nc):
            xb_ref[b, :, c * LANES:(c + 1) * LANES] = _load_token_chunk(
                raw_ref, buf_row(b), tm, nc, c).astype(BF16)

    def ffn_half():
        x = xb_ref[slot]
        gate = jnp.dot(x, wg_ref[...], preferred_element_type=F32) + bg_ref[0]
        up = jnp.dot(x, wu_ref[...], preferred_element_type=F32) + bu_ref[0]
        gate = jnp.minimum(gate, SWIGLU_LIMIT)
        up = jnp.clip(up, -SWIGLU_LIMIT, SWIGLU_LIMIT)
        hidden = (up + 1.0) * (gate * jax.nn.sigmoid(SWIGLU_ALPHA * gate))
        return jnp.dot(hidden.astype(BF16), wd_ref[...], preferred_element_type=F32)

    @pl.when((i == 0) & (j == 0))
    def _():
        acc_ref[1] = jnp.zeros((tm, d), F32)
        gather_loop(s0_ref, 0)
        gather_wait(0)
        slabs_to_lhs(0)
        gather_loop(s1_ref, 1)

    @pl.when(i < nu)
    def _():
        @pl.when(j == 0)
        def _():
            @pl.when(i >= 2)
            def _():
                scatter_wait(other)

            for r in range(tm):
                gather_one(s2_ref, slot, r)
            acc_ref[slot] = bd_ref[0] + ffn_half()
            _store_token_slabs(acc_ref[other], yst_ref, row0=other * tm)

        @pl.when(j == 1)
        def _():
            gather_wait(other)
            for r in range(tm):
                dst = jnp.where(i >= 1, dprev_ref[0, 0, r], n_tok * TOP_K + r)
                scatter_one(other, r, dst)
            acc_ref[slot] += ffn_half()
            slabs_to_lhs(other)

    @pl.when((i == nu) & (j == 0) & (nu > 0))
    def _():
        gather_wait(other)

        @pl.when(i >= 2)
        def _():
            scatter_wait(other)

        _store_token_slabs(acc_ref[other], yst_ref, row0=other * tm)

        def body(r, carry):
            scatter_one(other, r, dprev_ref[0, 0, r])
            return carry

        lax.fori_loop(0, tm, body, 0, unroll=8)
        scatter_wait(other)
        scatter_wait(slot)


def _experts(tile_expert, n_used, row_asg, h_slabs, w_gu, b_gu, w_dn, b_dn, tm, fc):
    n_exp, d, two_f = w_gu.shape
    nc = d // LANES
    n_tok = h_slabs.shape[0] // nc
    f = two_f // 2
    nf = f // fc
    n_tiles = row_asg.shape[0] // tm
    pad = row_asg >= n_tok * TOP_K
    row_tok = row_asg // TOP_K
    src3 = jnp.where(pad, 0, row_tok).reshape(n_tiles, 1, tm)
    spare = n_tok * TOP_K + jnp.arange(n_tiles * tm, dtype=I32) % tm
    dst3 = jnp.where(pad, spare, (row_asg % TOP_K) * n_tok + row_tok).reshape(n_tiles, 1, tm)

    assert nf == 2, "the expert kernel pipelines row traffic over exactly two hidden-dim halves"

    def tile(i, nu):
        return jnp.clip(i, 0, jnp.maximum(nu[0] - 1, 0))

    def chunk(i, j, nu):
        last = jnp.maximum(nu[0] - 1, 0)
        return jnp.where(i < nu[0], j ^ (i & 1), 1 ^ (last & 1))

    def smem_rows(off):
        return pl.BlockSpec((1, 1, tm), lambda i, j, te, nu: (tile(i + off, nu), 0, 0),
                            memory_space=pltpu.SMEM)

    grid_spec = pltpu.PrefetchScalarGridSpec(
        num_scalar_prefetch=2,
        grid=(n_tiles, nf),
        in_specs=[
            smem_rows(0),
            smem_rows(1),
            smem_rows(2),
            smem_rows(-1),
            pl.BlockSpec(memory_space=pl.ANY),
            pl.BlockSpec((None, d, fc), lambda i, j, te, nu: (te[tile(i, nu)], 0, chunk(i, j, nu))),
            pl.BlockSpec((None, d, fc), lambda i, j, te, nu: (te[tile(i, nu)], 0, nf + chunk(i, j, nu))),
            pl.BlockSpec((None, 1, fc), lambda i, j, te, nu: (te[tile(i, nu)], 0, chunk(i, j, nu))),
            pl.BlockSpec((None, 1, fc), lambda i, j, te, nu: (te[tile(i, nu)], 0, nf + chunk(i, j, nu))),
            pl.BlockSpec((None, fc, d), lambda i, j, te, nu: (te[tile(i, nu)], chunk(i, j, nu), 0)),
            pl.BlockSpec((None, 1, d), lambda i, j, te, nu: (te[tile(i, nu)], 0, 0)),
        ],
        out_specs=pl.BlockSpec(memory_space=pl.ANY),
        scratch_shapes=[
            pltpu.VMEM((2 * tm * nc, LANES), F32),
            pltpu.VMEM((2, tm, d), BF16),
            pltpu.VMEM((2, tm, d), F32),
            pltpu.VMEM((2 * tm * nc, LANES), F32),
            pltpu.SemaphoreType.DMA((2,)),
            pltpu.SemaphoreType.DMA((2,)),
        ],
    )
    return pl.pallas_call(
        functools.partial(_experts_body, n_tok=n_tok),
        grid_spec=grid_spec,
        out_shape=jax.ShapeDtypeStruct(((n_tok * TOP_K + tm) * nc, LANES), F32),
        compiler_params=pltpu.CompilerParams(
            dimension_semantics=("arbitrary", "arbitrary"),
            vmem_limit_bytes=EXPERTS_VMEM_LIMIT),
        name="experts",
    )(tile_expert, n_used, src3, src3, src3, dst3, h_slabs, w_gu, w_gu, b_gu, b_gu, w_dn, b_dn)


def _combine_body(x2_ref, pk_ref, *rest):
    y_refs, o_ref = rest[:TOP_K], rest[TOP_K]
    tm, d = x2_ref.shape
    nc = d // LANES
    pk = pk_ref[...]
    gates = [jnp.broadcast_to(pk[:, _PK_GATE + k:_PK_GATE + k + 1], (tm, LANES))
             for k in range(TOP_K)]
    for c in range(nc):
        acc = x2_ref[:, c * LANES:(c + 1) * LANES]
        for k in range(TOP_K):
            acc = acc + gates[k] * _load_token_chunk(y_refs[k], 0, tm, nc, c)
        o_ref[:, c * LANES:(c + 1) * LANES] = acc


def _combine(x2, pk, y, tm):
    t, d = x2.shape
    nc = d // LANES
    steps = t // tm

    def y_spec(k):
        return pl.BlockSpec((tm * nc, LANES), lambda i: (k * steps + i, 0))

    return pl.pallas_call(
        _combine_body,
        grid=(steps,),
        in_specs=[
            pl.BlockSpec((tm, d), lambda i: (i, 0)),
            pl.BlockSpec((tm, LANES), lambda i: (i, 0)),
        ] + [y_spec(k) for k in range(TOP_K)],
        out_specs=pl.BlockSpec((tm, d), lambda i: (i, 0)),
        out_shape=jax.ShapeDtypeStruct((t, d), F32),
        compiler_params=pltpu.CompilerParams(
            dimension_semantics=("parallel",), vmem_limit_bytes=VMEM_LIMIT),
        name="combine",
    )(x2, pk, *([y] * TOP_K))


def _tile_sizes(t):
    def pick(pref):
        while t % pref:
            pref //= 2
        return pref
    return dict(inproj=pick(1024), outproj=pick(512), expert=512, ffn_chunk=1024,
                combine=pick(256))


def _layer(x, g_norm_mix, w_in, b_in, g_q, g_k, sink, conv_w, conv_b, w_rg_a, b_rg_a,
           w_rg_x, b_rg_x, lam, g_attn_out, g_lru_out, w_out, g_norm_ffn, w_router, b_router,
           w_gate_up, b_gate_up, w_down, b_down):
    batch, seq, d = x.shape
    t = batch * seq
    ts = _tile_sizes(t)
    x2d = x.reshape(t, d)
    aw = N_Q_HEADS * HEAD_DIM
    kw = N_KV_HEADS * HEAD_DIM
    lw = conv_w.shape[1]
    nblk = lw // LRU_BLOCK_W
    n_exp = w_router.shape[1]

    proj = _inproj(x2d, g_norm_mix[None, :], w_in.astype(BF16), b_in[None, :], ts["inproj"], 512)

    attn, w_gu_bf16, w_dn_bf16 = _attention(proj, _rope_table(seq), g_q[None, :], g_k[None, :],
                                            sink, w_gate_up, w_down, batch, seq)

    w_cat = jnp.concatenate([w_rg_a[0], w_rg_x[0], w_rg_a[1], w_rg_x[1]], axis=-1).astype(BF16)
    bias_cat = jnp.concatenate(
        [b.reshape(nblk, 1, LRU_BLOCK_W) for b in (b_rg_a[0], b_rg_x[0], b_rg_a[1], b_rg_x[1])],
        axis=-1)
    rec = _lru(proj, conv_w, conv_b[None, :], w_cat, bias_cat, lam, batch, seq,
               x_col0=aw + 2 * kw, y_col0=aw + 2 * kw + lw)

    w_r = jnp.zeros((d, LANES), F32).at[:, :n_exp].set(w_router)
    w_r_hi = w_r.astype(BF16)
    w_r = jnp.concatenate([w_r_hi, (w_r - w_r_hi.astype(F32)).astype(BF16)], axis=1)
    b_r = jnp.zeros((1, LANES), F32).at[0, :n_exp].set(b_router)
    x2, h2, pk, cnt = _outproj(attn, rec, x2d, g_attn_out[None, :], g_lru_out[None, :],
                               w_out.astype(BF16), g_norm_ffn[None, :], w_r, b_r, n_exp,
                               ts["outproj"])

    tme = ts["expert"]
    idx = pk[:, _PK_IDX:_PK_IDX + TOP_K].astype(I32)
    rank = pk[:, _PK_RANK:_PK_RANK + TOP_K].astype(I32)
    counts = cnt[0, :n_exp].astype(I32)
    padded = (counts + tme - 1) // tme * tme
    pend = jnp.cumsum(padded)
    pstart = pend - padded
    dest = (pstart[idx] + rank).reshape(t * TOP_K)
    n_tiles = -(-(t * TOP_K) // tme) + n_exp
    tile_start = jnp.arange(n_tiles, dtype=I32) * tme
    tile_expert = jnp.minimum(
        jnp.sum((pend[None, :] <= tile_start[:, None]).astype(I32), axis=1), n_exp - 1)
    n_used = (pend[-1:] // tme).astype(I32)
    row_asg = jnp.full((n_tiles * tme,), t * TOP_K, I32).at[dest].set(
        jnp.arange(t * TOP_K, dtype=I32), unique_indices=True)

    f = w_gate_up.shape[-1] // 2
    y = _experts(tile_expert, n_used, row_asg, h2, w_gu_bf16,
                 b_gate_up.reshape(n_exp, 1, 2 * f), w_dn_bf16,
                 b_down.reshape(n_exp, 1, d), tme, ts["ffn_chunk"])
    out = _combine(x2, pk, y, ts["combine"])
    return out.reshape(batch, seq, d)


def kernel(x, g_norm_mix, w_in, b_in, g_q, g_k, sink, conv_w, conv_b, w_rg_a, b_rg_a, w_rg_x, b_rg_x, lam, g_attn_out, g_lru_out, w_out, g_norm_ffn, w_router, b_router, w_gate_up, b_gate_up, w_down, b_down):
    depth = w_in.shape[0]
    for l in range(depth):
        x = _layer(x, g_norm_mix[l], w_in[l], b_in[l], g_q[l], g_k[l], sink[l], conv_w[l],
                   conv_b[l], w_rg_a[l], b_rg_a[l], w_rg_x[l], b_rg_x[l], lam[l], g_attn_out[l],
                   g_lru_out[l], w_out[l], g_norm_ffn[l], w_router[l], b_router[l],
                   w_gate_up[l], b_gate_up[l], w_down[l], b_down[l])
    return x
```
